```python
import jax, jax.numpy as jnp
from jax import lax
import numpy as np

D_MODEL = 1024
BATCH = 32
SEQ = 2048
DEPTH = 4
DEC_BATCH = 32
DEC_SEQ = 16
PAST_LEN = 2048

CHUNK = 64
N_META = 16
N_HEADS = 8
N_KV_HEADS = 2
HEAD_DIM = 64
N_IDX_HEADS = 4
IDX_DIM = 64
TOP_K_MAX = 256
Q_BLOCK = 64
M_HEADS = 4
M_QK = 64
M_V = 128
M_CHUNK = 64
LRU_W = 512
LRU_BLOCKS = 8
LRU_BW = LRU_W // LRU_BLOCKS
CONV_W = 4
LRU_C = 8.0
N_BRANCH = 3
BRANCH_W = 512
D_FF = 4 * D_MODEL
EPS = 1e-6
NEG = -1e30

ATT_W = N_HEADS * HEAD_DIM
KV_W = N_KV_HEADS * HEAD_DIM
SPLIT_SIZES = (ATT_W, KV_W, KV_W, N_IDX_HEADS * IDX_DIM, IDX_DIM, N_IDX_HEADS,
               M_HEADS * M_QK, M_HEADS * M_QK, M_HEADS * M_V, M_HEADS, M_HEADS, M_HEADS * M_V,
               LRU_W, LRU_W, N_BRANCH * D_MODEL)
IN_W = sum(SPLIT_SIZES)

kernel_name = 'hybrid_streaming_encoder_step'


def _split_points():
    pts, acc = [], 0
    for s in SPLIT_SIZES[:-1]:
        acc += s
        pts.append(acc)
    return pts


def _rms(x, g):
    xf = x.astype(jnp.float32)
    y = xf * lax.rsqrt(jnp.mean(xf * xf, axis=-1, keepdims=True) + EPS)
    return (y * g.astype(jnp.float32)).astype(x.dtype)


def _chunk_ids(n):
    return (jnp.arange(n, dtype=jnp.int32) + (CHUNK - N_META)) // CHUNK


def _dsa_block(q, qi, wi, allowed, k, v, ki, top_k):
    B, Qb = q.shape[:2]
    f32 = jnp.float32
    sc = jax.nn.relu(jnp.einsum('bqhd,bld->bqhl', qi.astype(f32), ki.astype(f32)) * (IDX_DIM ** -0.5))
    sc = jnp.einsum('bqhl,bqh->bql', sc, wi.astype(f32))
    sc = jnp.where(allowed[None], sc, NEG)
    top_vals, idx = lax.top_k(sc, top_k)
    ok = top_vals > 0.5 * NEG
    gather = jax.vmap(lambda a, i: a[i])
    ks = gather(k, idx)
    vs = gather(v, idx)
    qg = q.reshape(B, Qb, N_KV_HEADS, N_HEADS // N_KV_HEADS, HEAD_DIM)
    s = jnp.einsum('bqkgd,bqskd->bqkgs', qg, ks).astype(f32) * (HEAD_DIM ** -0.5)
    s = jnp.where(ok[:, :, None, None, :], s, NEG)
    pr = jax.nn.softmax(s, axis=-1).astype(v.dtype)
    o = jnp.einsum('bqkgs,bqskd->bqkgd', pr, vs)
    return o.reshape(B, Qb, ATT_W)


def _dsa_prompt(q, k, v, qi, ki, wi, top_k):
    B, T = q.shape[:2]
    nb = -(-T // Q_BLOCK)
    pad = nb * Q_BLOCK - T
    padq = lambda a: jnp.pad(a, [(0, 0), (0, pad)] + [(0, 0)] * (a.ndim - 2))
    qp, qip, wip = padq(q), padq(qi), padq(wi)
    qid = _chunk_ids(nb * Q_BLOCK)
    kid = _chunk_ids(T)

    def one(bi):
        s0 = bi * Q_BLOCK
        sl = lambda a: lax.dynamic_slice_in_dim(a, s0, Q_BLOCK, axis=1)
        qc = lax.dynamic_slice_in_dim(qid, s0, Q_BLOCK)
        allowed = kid[None, :] <= qc[:, None]
        return _dsa_block(sl(qp), sl(qip), sl(wip), allowed, k, v, ki, top_k)

    out = lax.map(one, jnp.arange(nb))
    return jnp.moveaxis(out, 0, 1).reshape(B, nb * Q_BLOCK, ATT_W)[:, :T]


def _mlstm(q, k, v, i_pre, logf, C0, n0, m0):
    f32 = jnp.float32
    B, T = q.shape[:2]
    nc = -(-T // M_CHUNK)
    pad = nc * M_CHUNK - T

    def prep(a, fill):
        a = jnp.pad(a.astype(f32), [(0, 0), (0, pad)] + [(0, 0)] * (a.ndim - 2), constant_values=fill)
        return jnp.moveaxis(a.reshape((B, nc, M_CHUNK) + a.shape[2:]), 1, 0)

    xs = (prep(q, 0.0), prep(k, 0.0), prep(v, 0.0), prep(i_pre, NEG), prep(logf, 0.0))
    causal = jnp.tril(jnp.ones((M_CHUNK, M_CHUNK), bool))

    def step(carry, inp):
        C, n, m = carry
        qc, kc, vc, ic, fc = inp
        b = jnp.cumsum(fc, axis=1)
        log_d = b[:, :, None, :] - b[:, None, :, :] + ic[:, None, :, :]
        log_d = jnp.where(causal[None, :, :, None], log_d, NEG)
        m_t = jnp.maximum(b + m[:, None, :], jnp.max(log_d, axis=2))
        d = jnp.exp(log_d - m_t[:, :, None, :])
        inter = jnp.exp(b + m[:, None, :] - m_t)
        w = jnp.einsum('bthd,bshd->btsh', qc, kc) * d
        num = jnp.einsum('btsh,bshv->bthv', w, vc) + inter[..., None] * jnp.einsum('bhvd,bthd->bthv', C, qc)
        den = jnp.sum(w, axis=2) + inter * jnp.einsum('bhd,bthd->bth', n, qc)
        h = num / jnp.maximum(jnp.abs(den), jnp.exp(-m_t))[..., None]
        m_new = m_t[:, -1]
        decay = jnp.exp(b[:, -1] + m - m_new)
        ws = jnp.exp(b[:, -1:] - b + ic - m_new[:, None])
        C_new = decay[..., None, None] * C + jnp.einsum('bsh,bshv,bshd->bhvd', ws, vc, kc)
        n_new = decay[..., None] * n + jnp.einsum('bsh,bshd->bhd', ws, kc)
        return (C_new, n_new, m_new), h

    (C1, n1, m1), hs = lax.scan(step, (C0.astype(f32), n0.astype(f32), m0.astype(f32)), xs)
    h = jnp.moveaxis(hs, 0, 1).reshape(B, nc * M_CHUNK, M_HEADS, M_V)[:, :T]
    return h, C1, n1, m1


def _causal_conv(x, buf, w, b):
    T = x.shape[1]
    xp = jnp.concatenate([buf.astype(x.dtype), x], axis=1)
    y = b
    for j in range(CONV_W):
        y = y + xp[:, j:j + T] * w[j]
    return y, xp[:, T:]


def _rglru(xc, h0, wa, ba, wx, bx, lam):
    f32 = jnp.float32
    B, T, _ = xc.shape
    xf = xc.astype(f32)
    xb = xf.reshape(B, T, LRU_BLOCKS, LRU_BW)
    r = jax.nn.sigmoid(jnp.einsum('btnc,ncd->btnd', xb, wa.astype(f32)).reshape(B, T, LRU_W) + ba.astype(f32))
    gi = jax.nn.sigmoid(jnp.einsum('btnc,ncd->btnd', xb, wx.astype(f32)).reshape(B, T, LRU_W) + bx.astype(f32))
    log_a = -LRU_C * r * jax.nn.softplus(-lam.astype(f32))
    a = jnp.exp(log_a)
    u = jnp.sqrt(-jnp.expm1(2.0 * log_a)) * gi * xf
    u = u.at[:, 0].add(a[:, 0] * h0.astype(f32))

    def comb(l, rr):
        a1, b1 = l
        a2, b2 = rr
        return a1 * a2, a2 * b1 + b2

    _, h = lax.associative_scan(comb, (a, u), axis=1)
    return h, h[:, -1]


def _layer(x, p, st):
    f32 = jnp.float32
    B, T, _ = x.shape
    prompt = st is None
    hn = _rms(x, p['norm_mix'])
    z = hn @ p['w_in']
    (aq, ak, av, iq, ik, iw, mq, mk, mv, mi, mf, mo, lx, lg, gt) = jnp.split(z, _split_points(), axis=-1)

    q = _rms(aq.reshape(B, T, N_HEADS, HEAD_DIM), p['q_norm'])
    k = _rms(ak.reshape(B, T, N_KV_HEADS, HEAD_DIM), p['k_norm'])
    v = av.reshape(B, T, N_KV_HEADS, HEAD_DIM)
    qi = iq.reshape(B, T, N_IDX_HEADS, IDX_DIM)
    ki = _rms(ik, p['idx_k_norm'])
    wi = iw * (N_IDX_HEADS ** -0.5)
    if prompt:
        ya = _dsa_prompt(q, k, v, qi, ki, wi, min(TOP_K_MAX, (T - N_META) // 4))
    else:
        k_all = jnp.concatenate([st[0].astype(k.dtype), k], axis=1)
        v_all = jnp.concatenate([st[1].astype(v.dtype), v], axis=1)
        ki_all = jnp.concatenate([st[2].astype(ki.dtype), ki], axis=1)
        L = k_all.shape[1]
        allowed = jnp.ones((T, L), bool)
        ya = _dsa_block(q, qi, wi, allowed, k_all, v_all, ki_all, min(TOP_K_MAX, L // 4))

    mqh = mq.reshape(B, T, M_HEADS, M_QK)
    mkh = mk.reshape(B, T, M_HEADS, M_QK) * (M_QK ** -0.5)
    mvh = mv.reshape(B, T, M_HEADS, M_V)
    i_pre = mi.astype(f32) + p['m_bias_i'].astype(f32)
    logf = jax.nn.log_sigmoid(mf.astype(f32) + p['m_bias_f'].astype(f32))
    if prompt:
        C0 = jnp.zeros((B, M_HEADS, M_V, M_QK), f32)
        n0 = jnp.zeros((B, M_HEADS, M_QK), f32)
        m0 = jnp.zeros((B, M_HEADS), f32)
    else:
        C0, n0, m0 = st[3], st[4], st[5]
    hm, C1, n1, m1 = _mlstm(mqh, mkh, mvh, i_pre, logf, C0, n0, m0)
    yb = _rms(hm, p['m_norm']).reshape(B, T, M_HEADS * M_V).astype(x.dtype) * jax.nn.sigmoid(mo)

    buf = jnp.zeros((B, CONV_W - 1, LRU_W), x.dtype) if prompt else st[7]
    h0 = jnp.zeros((B, LRU_W), f32) if prompt else st[6]
    xc, buf1 = _causal_conv(lx, buf, p['conv_w'], p['conv_b'])
    hl, h1 = _rglru(xc, h0, p['lru_wa'], p['lru_ba'], p['lru_wx'], p['lru_bx'], p['lru_lambda'])
    yc = hl.astype(x.dtype) * jax.nn.gelu(lg)

    br = jnp.stack([ya.astype(x.dtype), yb, yc], axis=2)
    proj = jnp.einsum('btnc,ncd->btnd', br, p['w_branch'])
    g = jax.nn.sigmoid(gt.reshape(B, T, N_BRANCH, D_MODEL))
    x = x + jnp.sum(g * proj, axis=2) @ p['w_out']

    hf = _rms(x, p['norm_ffn'])
    x = x + jnp.square(jax.nn.relu(hf @ p['w_up'])) @ p['w_down']
    return x, (k, v, ki, C1, n1, m1, h1, buf1)


def setup_inputs(seed: int = 0) -> dict:
    key = jax.random.key(seed)
    ks = jax.random.split(key, 32)
    f32 = jnp.float32
    nrm = lambda kk, shape, scale: scale * jax.random.normal(kk, shape, f32)
    u = jax.random.uniform(ks[29], (DEPTH, LRU_W), f32, 0.9, 0.999)
    sa = u ** (1.0 / LRU_C)
    lru_lambda = jnp.log(sa) - jnp.log1p(-sa)
    return {
        'x_prompt': nrm(ks[0], (BATCH, SEQ, D_MODEL), 1.0),
        'x_sample': nrm(ks[1], (DEC_BATCH, DEC_SEQ, D_MODEL), 1.0),
        'cache_k': nrm(ks[2], (DEPTH, DEC_BATCH, PAST_LEN, N_KV_HEADS, HEAD_DIM), 1.0),
        'cache_v': nrm(ks[3], (DEPTH, DEC_BATCH, PAST_LEN, N_KV_HEADS, HEAD_DIM), 1.0),
        'cache_idx_k': nrm(ks[4], (DEPTH, DEC_BATCH, PAST_LEN, IDX_DIM), 1.0),
        'state_mlstm_c': nrm(ks[5], (DEPTH, DEC_BATCH, M_HEADS, M_V, M_QK), 0.5),
        'state_mlstm_n': nrm(ks[6], (DEPTH, DEC_BATCH, M_HEADS, M_QK), 0.5),
        'state_mlstm_m': nrm(ks[7], (DEPTH, DEC_BATCH, M_HEADS), 0.5),
        'state_lru_h': nrm(ks[8], (DEPTH, DEC_BATCH, LRU_W), 0.5),
        'state_conv': nrm(ks[9], (DEPTH, DEC_BATCH, CONV_W - 1, LRU_W), 1.0),
        'meta_tokens': nrm(ks[10], (N_META, D_MODEL), 1.0),
        'norm_mix': 1.0 + nrm(ks[11], (DEPTH, D_MODEL), 0.02),
        'w_in': nrm(ks[12], (DEPTH, D_MODEL, IN_W), D_MODEL ** -0.5),
        'q_norm': 1.0 + nrm(ks[13], (DEPTH, HEAD_DIM), 0.02),
        'k_norm': 1.0 + nrm(ks[14], (DEPTH, HEAD_DIM), 0.02),
        'idx_k_norm': 1.0 + nrm(ks[15], (DEPTH, IDX_DIM), 0.02),
        'm_bias_i': nrm(ks[16], (DEPTH, M_HEADS), 0.1),
        'm_bias_f': 3.0 + nrm(ks[17], (DEPTH, M_HEADS), 0.5),
        'm_norm': 1.0 + nrm(ks[18], (DEPTH, M_HEADS, M_V), 0.02),
        'conv_w': nrm(ks[19], (DEPTH, CONV_W, LRU_W), CONV_W ** -0.5),
        'conv_b': nrm(ks[20], (DEPTH, LRU_W), 0.02),
        'lru_wa': nrm(ks[21], (DEPTH, LRU_BLOCKS, LRU_BW, LRU_BW), LRU_BW ** -0.5),
        'lru_ba': nrm(ks[22], (DEPTH, LRU_W), 0.02),
        'lru_wx': nrm(ks[23], (DEPTH, LRU_BLOCKS, LRU_BW, LRU_BW), LRU_BW ** -0.5),
        'lru_bx': nrm(ks[24], (DEPTH, LRU_W), 0.02),
        'lru_lambda': lru_lambda,
        'w_branch': nrm(ks[25], (DEPTH, N_BRANCH, BRANCH_W, D_MODEL), BRANCH_W ** -0.5),
        'w_out': nrm(ks[26], (DEPTH, D_MODEL, D_MODEL), D_MODEL ** -0.5),
        'norm_ffn': 1.0 + nrm(ks[27], (DEPTH, D_MODEL), 0.02),
        'w_up': nrm(ks[28], (DEPTH, D_MODEL, D_FF), D_MODEL ** -0.5),
        'w_down': nrm(ks[30], (DEPTH, D_FF, D_MODEL), D_FF ** -0.5),
        'final_norm': 1.0 + nrm(ks[31], (D_MODEL,), 0.02),
    }


def reference(x_prompt, x_sample, cache_k, cache_v, cache_idx_k, state_mlstm_c, state_mlstm_n,
              state_mlstm_m, state_lru_h, state_conv, meta_tokens, norm_mix, w_in, q_norm, k_norm,
              idx_k_norm, m_bias_i, m_bias_f, m_norm, conv_w, conv_b, lru_wa, lru_ba, lru_wx, lru_bx,
              lru_lambda, w_branch, w_out, norm_ffn, w_up, w_down, final_norm):
    layers = [dict(norm_mix=norm_mix[l], w_in=w_in[l], q_norm=q_norm[l], k_norm=k_norm[l],
                   idx_k_norm=idx_k_norm[l], m_bias_i=m_bias_i[l], m_bias_f=m_bias_f[l], m_norm=m_norm[l],
                   conv_w=conv_w[l], conv_b=conv_b[l], lru_wa=lru_wa[l], lru_ba=lru_ba[l],
                   lru_wx=lru_wx[l], lru_bx=lru_bx[l], lru_lambda=lru_lambda[l], w_branch=w_branch[l],
                   w_out=w_out[l], norm_ffn=norm_ffn[l], w_up=w_up[l], w_down=w_down[l])
              for l in range(DEPTH)]
    B = x_prompt.shape[0]
    meta = jnp.broadcast_to(meta_tokens[None].astype(x_prompt.dtype), (B, N_META, D_MODEL))
    hp = jnp.concatenate([meta, x_prompt], axis=1)
    hs = x_sample
    p_states, s_states = [], []
    for l in range(DEPTH):
        hp, stp = _layer(hp, layers[l], None)
        hs, sts = _layer(hs, layers[l], (cache_k[l], cache_v[l], cache_idx_k[l], state_mlstm_c[l],
                                         state_mlstm_n[l], state_mlstm_m[l], state_lru_h[l], state_conv[l]))
        p_states.append(stp)
        s_states.append(sts)
    y_prompt = _rms(hp[:, N_META:], final_norm)
    y_sample = _rms(hs, final_norm)
    pk, pv, pki, pc, pn, pm, ph, pcv = [jnp.stack(t) for t in zip(*p_states)]
    sk, sv, ski, sc, sn, sm, sh, scv = [jnp.stack(t) for t in zip(*s_states)]
    return (y_prompt, y_sample, pk, pv, pki, pc, pn, pm, ph, pcv, sk, sv, ski, sc, sn, sm, sh, scv)
```

```python
import functools

import numpy as np
import jax
import jax.numpy as jnp
from jax import lax
from jax.experimental import pallas as pl
from jax.experimental.pallas import tpu as pltpu

F32 = jnp.float32
BF16 = jnp.bfloat16
I32 = jnp.int32

CHUNK = 64
N_META = 16
N_HEADS = 8
N_KV_HEADS = 2
HEAD_DIM = 64
N_IDX_HEADS = 4
IDX_DIM = 64
TOP_K_MAX = 256
M_HEADS = 4
M_QK = 64
M_V = 128
LRU_W = 512
LRU_BLOCKS = 8
CONV_W = 4
LRU_C = 8.0
N_BRANCH = 3
BRANCH_W = 512
EPS = 1e-6
NEG = -1e30

ATT_W = N_HEADS * HEAD_DIM
KV_W = N_KV_HEADS * HEAD_DIM
GROUP = N_HEADS // N_KV_HEADS

LANES = 128
ROW_PAD = 128
VMEM_LIMIT = 56 * 1024 * 1024

SM_IK = 0
SM_IW = IDX_DIM
SM_MI = SM_IW + N_IDX_HEADS
SM_MF = SM_MI + M_HEADS
SM_W = 128


def _z_layout(d_model):
    widths = [("gt", N_BRANCH * d_model), ("aq", ATT_W), ("mv", M_HEADS * M_V), ("mo", M_HEADS * M_V),
              ("lx", LRU_W), ("lg", LRU_W), ("iq", N_IDX_HEADS * IDX_DIM), ("mq", M_HEADS * M_QK),
              ("mk", M_HEADS * M_QK), ("ak", KV_W), ("av", KV_W), ("sm", SM_W)]
    off, acc = {}, 0
    for name, w in widths:
        assert acc % w == 0, (name, acc, w)
        off[name] = (acc, w)
        acc += w
    return off, acc


def _pack_w_in(w_in, d_model, n_col_tiles):
    sizes = (ATT_W, KV_W, KV_W, N_IDX_HEADS * IDX_DIM, IDX_DIM, N_IDX_HEADS,
             M_HEADS * M_QK, M_HEADS * M_QK, M_HEADS * M_V, M_HEADS, M_HEADS, M_HEADS * M_V,
             LRU_W, LRU_W, N_BRANCH * d_model)
    names = ("aq", "ak", "av", "iq", "ik", "iw", "mq", "mk", "mv", "mi", "mf", "mo", "lx", "lg", "gt")
    assert w_in.shape[-1] == sum(sizes)
    parts, acc = {}, 0
    for n, s in zip(names, sizes):
        parts[n] = w_in[..., acc:acc + s]
        acc += s
    lead = w_in.shape[:-1]
    small = jnp.concatenate([parts["ik"], parts["iw"], parts["mi"], parts["mf"],
                             jnp.zeros(lead + (SM_W - SM_MF - M_HEADS,), w_in.dtype)], axis=-1)
    parts["sm"] = small
    off, zw = _z_layout(d_model)
    order = sorted(off, key=lambda n: off[n][0])
    cols = [parts[n] for n in order]
    zw_pad = -(-zw // (n_col_tiles * LANES)) * (n_col_tiles * LANES)
    if zw_pad > zw:
        cols.append(jnp.zeros(lead + (zw_pad - zw,), w_in.dtype))
    return jnp.concatenate(cols, axis=-1).astype(BF16), off, zw_pad


def _params(*sem):
    return pltpu.CompilerParams(dimension_semantics=sem, vmem_limit_bytes=VMEM_LIMIT)


def _bdot(a, b):
    return jnp.dot(a.astype(BF16), b.astype(BF16), preferred_element_type=F32)


def _bdot_nt(a, b):
    return lax.dot_general(a.astype(BF16), b.astype(BF16), (((1,), (1,)), ((), ())),
                           preferred_element_type=F32)


def _split3(a):
    a1 = a.astype(BF16)
    r1 = a - a1.astype(F32)
    a2 = r1.astype(BF16)
    a3 = (r1 - a2.astype(F32)).astype(BF16)
    return a1, a2, a3


def _dot3_nt(a, b):
    a1, a2, _ = _split3(a)
    b1, b2, _ = _split3(b)
    dn = (((1,), (1,)), ((), ()))
    f = lambda x, y: lax.dot_general(x, y, dn, preferred_element_type=F32)
    return f(a1, b1) + (f(a1, b2) + f(a2, b1))


def _dot3(a, b):
    a1, a2, _ = _split3(a)
    b1, b2, _ = _split3(b)
    f = lambda x, y: jnp.dot(x, y, preferred_element_type=F32)
    return f(a1, b1) + (f(a1, b2) + f(a2, b1))


def _rms_rows(x, g):
    return x * lax.rsqrt(jnp.mean(x * x, axis=-1, keepdims=True) + EPS) * g


def _sigmoid(x):
    return 1.0 / (1.0 + jnp.exp(-x))


def _log_sigmoid(x):
    return jnp.minimum(x, 0.0) - jnp.log1p(jnp.exp(-jnp.abs(x)))


def _softplus(x):
    return jnp.maximum(x, 0.0) + jnp.log1p(jnp.exp(-jnp.abs(x)))


def _gelu_tanh(x):
    c = np.float32(np.sqrt(2.0 / np.pi))
    return 0.5 * x * (1.0 + jnp.tanh(c * (x + np.float32(0.044715) * (x * x * x))))


def _inproj_kernel(x_ref, g_ref, w_ref, z_ref):
    hn = _rms_rows(x_ref[...], g_ref[...])
    z_ref[...] = jnp.dot(hn.astype(BF16), w_ref[...], preferred_element_type=F32)


def _inproj(x, g, w_packed, layer, *, tm, n_col_tiles):
    rows, d = x.shape
    zw = w_packed.shape[-1]
    tn = zw // n_col_tiles
    return pl.pallas_call(
        _inproj_kernel,
        grid=(n_col_tiles, rows // tm),
        in_specs=[pl.BlockSpec((tm, d), lambda j, i: (i, 0)),
                  pl.BlockSpec((None, 1, d), lambda j, i: (layer, 0, 0)),
                  pl.BlockSpec((None, d, tn), lambda j, i: (layer, 0, j))],
        out_specs=pl.BlockSpec((tm, tn), lambda j, i: (i, j)),
        out_shape=jax.ShapeDtypeStruct((rows, zw), F32),
        compiler_params=_params("parallel", "parallel"),
        name="inproj",
    )(x, g, w_packed)


def _kvprep_kernel(ak_ref, av_ref, sm_ref, kn_ref, ikn_ref, k_ref, v_ref, ki_ref):
    ak = ak_ref[...]
    kn = kn_ref[...]
    ks = [_rms_rows(ak[:, h * HEAD_DIM:(h + 1) * HEAD_DIM], kn) for h in range(N_KV_HEADS)]
    k_ref[...] = jnp.concatenate(ks, axis=-1)
    v_ref[...] = av_ref[...]
    ki_ref[...] = _rms_rows(sm_ref[...][:, SM_IK:SM_IK + IDX_DIM], ikn_ref[...])


def _kvprep(z, off, k_norm, idx_k_norm, layer, *, tm):
    rows = z.shape[0]
    cb = lambda name: off[name][0] // off[name][1]
    return pl.pallas_call(
        _kvprep_kernel,
        grid=(rows // tm,),
        in_specs=[pl.BlockSpec((tm, KV_W), lambda i: (i, cb("ak"))),
                  pl.BlockSpec((tm, KV_W), lambda i: (i, cb("av"))),
                  pl.BlockSpec((tm, SM_W), lambda i: (i, cb("sm"))),
                  pl.BlockSpec((None, 1, HEAD_DIM), lambda i: (layer, 0, 0)),
                  pl.BlockSpec((None, 1, IDX_DIM), lambda i: (layer, 0, 0))],
        out_specs=[pl.BlockSpec((tm, KV_W), lambda i: (i, 0)),
                   pl.BlockSpec((tm, KV_W), lambda i: (i, 0)),
                   pl.BlockSpec((tm, IDX_DIM), lambda i: (i, 0))],
        out_shape=[jax.ShapeDtypeStruct((rows, KV_W), F32),
                   jax.ShapeDtypeStruct((rows, KV_W), F32),
                   jax.ShapeDtypeStruct((rows, IDX_DIM), F32)],
        compiler_params=_params("parallel"),
        name="kvprep",
    )(z, z, z, k_norm, idx_k_norm)


_INT_MIN = np.int32(-2 ** 31)


def _sort_key_of(x):
    b = np.array(x, np.float32).view(np.int32)
    return np.int32(b ^ 0x7FFFFFFF) if b < 0 else np.int32(b)


_OK_KEY = _sort_key_of(0.5 * NEG)


def _dsa_kernel(aq_ref, iq_ref, sm_ref, k_ref, v_ref, ki_ref, qn_ref, o_ref, key_ref, msk_ref,
                *, tq, lp, l_valid, top_k, prompt):
    nblk = lp // LANES
    kf = np.float32(top_k)

    iq = iq_ref[...]
    sm = sm_ref[...]
    ki = ki_ref[...]
    sc = None
    for h in range(N_IDX_HEADS):
        s = _dot3_nt(iq[:, h * IDX_DIM:(h + 1) * IDX_DIM], ki)
        s = jnp.maximum(s * np.float32(IDX_DIM ** -0.5), 0.0)
        w = sm[:, SM_IW + h:SM_IW + h + 1] * np.float32(N_IDX_HEADS ** -0.5)
        sc = s * w if sc is None else sc + s * w
    sc = jnp.where(sc == 0.0, 0.0, sc)

    kpos = lax.broadcasted_iota(I32, (1, lp), 1)
    if prompt:
        qpos = pl.program_id(1) * tq + lax.broadcasted_iota(I32, (tq, 1), 0)
        allowed = ((kpos + (CHUNK - N_META)) // CHUNK <= (qpos + (CHUNK - N_META)) // CHUNK) & (kpos < l_valid)
    else:
        allowed = jnp.broadcast_to(kpos < l_valid, (tq, lp))
    scm = jnp.where(allowed, sc, np.float32(NEG))
    bits = pltpu.bitcast(scm, I32)
    key_ref[...] = jnp.where(bits < 0, bits ^ np.int32(0x7FFFFFFF), bits)

    def count_ge(cand):
        acc = jnp.zeros((tq, LANES), F32)
        for j in range(nblk):
            acc = acc + jnp.where(key_ref[:, j * LANES:(j + 1) * LANES] >= cand, 1.0, 0.0)
        return jnp.sum(acc, axis=-1, keepdims=True)

    ans = jnp.where(count_ge(jnp.zeros((tq, 1), I32)) >= kf, np.int32(0), _INT_MIN)

    def bit_step(t, ans):
        cand = ans | lax.shift_left(np.int32(1), np.int32(30) - t)
        return jnp.where(count_ge(cand) >= kf, cand, ans)

    thr = lax.fori_loop(0, 31, bit_step, ans)

    acc = jnp.zeros((tq, LANES), F32)
    for j in range(nblk):
        acc = acc + jnp.where(key_ref[:, j * LANES:(j + 1) * LANES] > thr, 1.0, 0.0)
    need = kf - jnp.sum(acc, axis=-1, keepdims=True)
    tri = (lax.broadcasted_iota(I32, (LANES, LANES), 0) < lax.broadcasted_iota(I32, (LANES, LANES), 1))
    tri = jnp.where(tri, 1.0, 0.0).astype(BF16)
    seen = jnp.zeros((tq, 1), F32)
    for j in range(nblk):
        kj = key_ref[:, j * LANES:(j + 1) * LANES]
        eq = jnp.where(kj == thr, 1.0, 0.0)
        before = seen + jnp.dot(eq.astype(BF16), tri, preferred_element_type=F32)
        sel = (kj > thr) | ((kj == thr) & (before < need))
        sel = sel & (kj > _OK_KEY)
        msk_ref[:, j * LANES:(j + 1) * LANES] = jnp.where(sel, 1.0, 0.0)
        seen = seen + jnp.sum(eq, axis=-1, keepdims=True)

    q = aq_ref[...]
    qn = qn_ref[...]
    kk = k_ref[...]
    vv = v_ref[...]
    sel = msk_ref[...] > 0.5
    outs = []
    for g in range(N_KV_HEADS):
        kg = kk[:, g * HEAD_DIM:(g + 1) * HEAD_DIM].astype(BF16)
        vg = vv[:, g * HEAD_DIM:(g + 1) * HEAD_DIM].astype(BF16)
        for hh in range(GROUP):
            h = g * GROUP + hh
            qh = _rms_rows(q[:, h * HEAD_DIM:(h + 1) * HEAD_DIM], qn)
            s = _bdot_nt(qh, kg) * np.float32(HEAD_DIM ** -0.5)
            s = jnp.where(sel, s, np.float32(NEG))
            p = jnp.exp(s - jnp.max(s, axis=-1, keepdims=True))
            den = jnp.sum(p, axis=-1, keepdims=True)
            outs.append(jnp.dot(p.astype(BF16), vg, preferred_element_type=F32) / den)
    o_ref[...] = jnp.concatenate(outs, axis=-1)


def _dsa(z, off, kmat, vmat, kimat, q_norm, layer, *, n_batch, rows_per_batch, tq, n_q, l_valid, top_k, prompt):
    lp = kmat.shape[1]
    assert lp % LANES == 0 and rows_per_batch % tq == 0
    rb = rows_per_batch // tq
    cb = lambda name: off[name][0] // off[name][1]
    kern = functools.partial(_dsa_kernel, tq=tq, lp=lp, l_valid=l_valid, top_k=top_k, prompt=prompt)
    return pl.pallas_call(
        kern,
        grid=(n_batch, n_q),
        in_specs=[pl.BlockSpec((tq, ATT_W), lambda b, i: (b * rb + i, cb("aq"))),
                  pl.BlockSpec((tq, N_IDX_HEADS * IDX_DIM), lambda b, i: (b * rb + i, cb("iq"))),
                  pl.BlockSpec((tq, SM_W), lambda b, i: (b * rb + i, cb("sm"))),
                  pl.BlockSpec((None, lp, KV_W), lambda b, i: (b, 0, 0)),
                  pl.BlockSpec((None, lp, KV_W), lambda b, i: (b, 0, 0)),
                  pl.BlockSpec((None, lp, IDX_DIM), lambda b, i: (b, 0, 0)),
                  pl.BlockSpec((None, 1, HEAD_DIM), lambda b, i: (layer, 0, 0))],
        out_specs=pl.BlockSpec((tq, ATT_W), lambda b, i: (b * n_q + i, 0)),
        out_shape=jax.ShapeDtypeStruct((n_batch * n_q * tq, ATT_W), F32),
        scratch_shapes=[pltpu.VMEM((tq, lp), I32), pltpu.VMEM((tq, lp), F32)],
        compiler_params=_params("parallel", "parallel"),
        name="dsa",
    )(z, z, z, kmat, vmat, kimat, q_norm)


def _mlstm_kernel(bi_ref, bf_ref, mq_ref, mk_ref, mv_ref, mo_ref, sm_ref, gn_ref, c0_ref, n0_ref, m0_ref,
                  yb_ref, c1_ref, n1_ref, m1_ref, c_s, n_s, m_s, *, lc, t_valid, layer):
    j = pl.program_id(1)

    @pl.when(j == 0)
    def _():
        c_s[...] = c0_ref[...]
        n_s[...] = n0_ref[...]
        m_s[...] = m0_ref[...]

    sm = sm_ref[...]
    smt = sm.T
    t_io = lax.broadcasted_iota(I32, (lc, lc), 0)
    s_io = lax.broadcasted_iota(I32, (lc, lc), 1)
    causal = s_io <= t_io
    pos_col = j * lc + lax.broadcasted_iota(I32, (lc, 1), 0)
    pos_row = j * lc + lax.broadcasted_iota(I32, (1, lc), 1)
    lane = lax.broadcasted_iota(I32, (1, LANES), 1)
    mq = mq_ref[...]
    mk = mk_ref[...]
    mv = mv_ref[...]
    mo = mo_ref[...]
    gn = gn_ref[...]
    m_all = m_s[...]
    m_out = m_all
    for h in range(M_HEADS):
        bi = bi_ref[layer, h]
        bf = bf_ref[layer, h]
        i_col = jnp.where(pos_col < t_valid, sm[:, SM_MI + h:SM_MI + h + 1] + bi, np.float32(NEG))
        f_col = jnp.where(pos_col < t_valid, _log_sigmoid(sm[:, SM_MF + h:SM_MF + h + 1] + bf), 0.0)
        i_row = jnp.where(pos_row < t_valid, smt[SM_MI + h:SM_MI + h + 1, :] + bi, np.float32(NEG))
        f_row = jnp.where(pos_row < t_valid, _log_sigmoid(smt[SM_MF + h:SM_MF + h + 1, :] + bf), 0.0)
        b_col = jnp.sum(jnp.where(causal, f_row, 0.0), axis=1, keepdims=True)
        b_row = jnp.sum(jnp.where(t_io <= s_io, f_col, 0.0), axis=0, keepdims=True)
        m_prev = m_all[:, h:h + 1]
        log_d = jnp.where(causal, b_col - b_row + i_row, np.float32(NEG))
        m_t = jnp.maximum(b_col + m_prev, jnp.max(log_d, axis=1, keepdims=True))
        d = jnp.exp(log_d - m_t)
        inter = jnp.exp(b_col + m_prev - m_t)
        qh = mq[:, h * M_QK:(h + 1) * M_QK]
        kh = mk[:, h * M_QK:(h + 1) * M_QK] * np.float32(M_QK ** -0.5)
        vh = mv[:, h * M_V:(h + 1) * M_V]
        c_h = c_s[h]
        n_h = n_s[h]
        w = _bdot_nt(qh, kh) * d
        num = _bdot(w, vh) + inter * _bdot_nt(qh, c_h)
        den = jnp.sum(w, axis=1, keepdims=True) + inter * jnp.sum(qh * n_h, axis=1, keepdims=True)
        hm = num / jnp.maximum(jnp.abs(den), jnp.exp(-m_t))
        m_new = m_t[lc - 1:lc, :]
        b_last = b_col[lc - 1:lc, :]
        decay = jnp.exp(b_last + m_prev - m_new)
        ws = jnp.exp(b_last - b_col + i_col - m_new)
        c_s[h] = decay * c_h + _bdot((vh * ws).T, kh)
        n_s[h] = decay * n_h + jnp.sum(ws * kh, axis=0, keepdims=True)
        m_out = jnp.where(lane == h, m_new, m_out)
        y = _rms_rows(hm, gn[:, h * M_V:(h + 1) * M_V])
        yb_ref[:, h * M_V:(h + 1) * M_V] = y * _sigmoid(mo[:, h * M_V:(h + 1) * M_V])
    m_s[...] = m_out

    @pl.when(j == pl.num_programs(1) - 1)
    def _():
        c1_ref[...] = c_s[...]
        n1_ref[...] = n_s[...]
        m1_ref[...] = m_s[...]


def _mlstm(z, off, bias_i, bias_f, m_norm, c0, n0, m0, layer, *, n_batch, tp, t_valid, lc):
    nc = tp // lc
    cb = lambda name: off[name][0] // off[name][1]
    row = lambda name: pl.BlockSpec((lc, off[name][1]), lambda b, j: (b * nc + j, cb(name)))
    smem = pl.BlockSpec(memory_space=pltpu.SMEM)
    st_c = pl.BlockSpec((None, M_HEADS, M_V, M_QK), lambda b, j: (b, 0, 0, 0))
    st_n = pl.BlockSpec((None, M_HEADS, 1, M_QK), lambda b, j: (b, 0, 0, 0))
    st_m = pl.BlockSpec((None, 1, LANES), lambda b, j: (b, 0, 0))
    kern = functools.partial(_mlstm_kernel, lc=lc, t_valid=t_valid, layer=layer)
    return pl.pallas_call(
        kern,
        grid=(n_batch, nc),
        in_specs=[smem, smem, row("mq"), row("mk"), row("mv"), row("mo"), row("sm"),
                  pl.BlockSpec((None, 1, M_HEADS * M_V), lambda b, j: (layer, 0, 0)),
                  st_c, st_n, st_m],
        out_specs=[pl.BlockSpec((lc, M_HEADS * M_V), lambda b, j: (b * nc + j, 0)), st_c, st_n, st_m],
        out_shape=[jax.ShapeDtypeStruct((n_batch * tp, M_HEADS * M_V), F32),
                   jax.ShapeDtypeStruct(c0.shape, F32), jax.ShapeDtypeStruct(n0.shape, F32),
                   jax.ShapeDtypeStruct(m0.shape, F32)],
        scratch_shapes=[pltpu.VMEM((M_HEADS, M_V, M_QK), F32), pltpu.VMEM((M_HEADS, 1, M_QK), F32),
                        pltpu.VMEM((1, LANES), F32)],
        compiler_params=_params("parallel", "arbitrary"),
        name="mlstm",
    )(bias_i, bias_f, z, z, z, z, z, m_norm, c0, n0, m0)


def _shift_rows(x, s, fill):
    rolled = pltpu.roll(x, s, 0)
    row = lax.broadcasted_iota(I32, x.shape, 0)
    return jnp.where(row >= s, rolled, fill)


def _lru_kernel(lx_ref, lg_ref, cw_ref, cb_ref, wa_ref, ba_ref, wx_ref, bx_ref, lam_ref, h0_ref, buf0_ref,
                yc_ref, h1_ref, buf1_ref, prev_s, h_s, *, tt, t_valid):
    j = pl.program_id(1)

    @pl.when(j == 0)
    def _():
        prev_s[...] = buf0_ref[...]
        h_s[...] = h0_ref[...]

    lx = lx_ref[...]
    cw = cw_ref[...]
    prev = prev_s[...]
    row8 = lax.broadcasted_iota(I32, prev.shape, 0)
    xc = cb_ref[...] + lx * cw[CONV_W - 1:CONV_W, :]
    for s in range(1, CONV_W):
        rolled = pltpu.roll(lx, s, 0)
        head = jnp.where(row8 < s, pltpu.roll(prev, s, 0), rolled[:8])
        shifted = jnp.concatenate([head, rolled[8:]], axis=0) if tt > 8 else head
        xc = xc + shifted * cw[CONV_W - 1 - s:CONV_W - s, :]
    prev_s[...] = lx[tt - 8:, :]

    r = _sigmoid(_dot3(xc, wa_ref[...]) + ba_ref[...])
    gi = _sigmoid(_dot3(xc, wx_ref[...]) + bx_ref[...])
    log_a = np.float32(-LRU_C) * r * _softplus(-lam_ref[...])
    a = jnp.exp(log_a)
    u = jnp.sqrt(jnp.tanh(-log_a) * (a * a + 1.0)) * gi * xc
    s = 1
    while s < tt:
        u = a * _shift_rows(u, s, 0.0) + u
        a = a * _shift_rows(a, s, 1.0)
        s *= 2
    h = a * h_s[...] + u
    h_s[...] = h[tt - 1:tt, :]
    yc_ref[...] = h * _gelu_tanh(lg_ref[...])

    last = t_valid - 1

    @pl.when(j == last // tt)
    def _():
        r0 = last % tt
        h1_ref[...] = h[r0:r0 + 1, :]
        buf1_ref[...] = lx[r0 - 7:r0 + 1, :]


def _lru(z, off, conv_w, conv_b, wa_bd, ba, wx_bd, bx, lam, h0, buf0, layer, *, n_batch, tp, t_valid, tt):
    assert t_valid % 8 == 0 and tt % 8 == 0
    nt = tp // tt
    cb = lambda name: off[name][0] // off[name][1]
    row = lambda name: pl.BlockSpec((tt, LRU_W), lambda b, j: (b * nt + j, cb(name)))
    vec = lambda n: pl.BlockSpec((None, n, LRU_W), lambda b, j: (layer, 0, 0))
    st_h = pl.BlockSpec((None, 1, LRU_W), lambda b, j: (b, 0, 0))
    st_b = pl.BlockSpec((None, 8, LRU_W), lambda b, j: (b, 0, 0))
    kern = functools.partial(_lru_kernel, tt=tt, t_valid=t_valid)
    return pl.pallas_call(
        kern,
        grid=(n_batch, nt),
        in_specs=[row("lx"), row("lg"), vec(CONV_W), vec(1), vec(LRU_W), vec(1), vec(LRU_W), vec(1), vec(1),
                  st_h, st_b],
        out_specs=[pl.BlockSpec((tt, LRU_W), lambda b, j: (b * nt + j, 0)), st_h, st_b],
        out_shape=[jax.ShapeDtypeStruct((n_batch * tp, LRU_W), F32),
                   jax.ShapeDtypeStruct(h0.shape, F32), jax.ShapeDtypeStruct(buf0.shape, F32)],
        scratch_shapes=[pltpu.VMEM((8, LRU_W), F32), pltpu.VMEM((1, LRU_W), F32)],
        compiler_params=_params("parallel", "arbitrary"),
        name="lru",
    )(z, z, conv_w, conv_b, wa_bd, ba, wx_bd, bx, lam, h0, buf0)


def _merge_kernel(x_ref, ya_ref, yb_ref, yc_ref, gt_ref, wb_ref, wo_ref, o_ref, *, d):
    acc = None
    for b, y_ref in enumerate((ya_ref, yb_ref, yc_ref)):
        proj = jnp.dot(y_ref[...].astype(BF16), wb_ref[b], preferred_element_type=F32)
        t = _sigmoid(gt_ref[:, b * d:(b + 1) * d]) * proj
        acc = t if acc is None else acc + t
    o_ref[...] = x_ref[...] + jnp.dot(acc.astype(BF16), wo_ref[...], preferred_element_type=F32)


def _merge(x, ya, yb, yc, z, off, w_branch, w_out, layer, *, tm):
    rows, d = x.shape
    assert off["gt"][0] == 0
    rowspec = lambda w: pl.BlockSpec((tm, w), lambda i: (i, 0))
    return pl.pallas_call(
        functools.partial(_merge_kernel, d=d),
        grid=(rows // tm,),
        in_specs=[rowspec(d), rowspec(BRANCH_W), rowspec(BRANCH_W), rowspec(BRANCH_W), rowspec(N_BRANCH * d),
                  pl.BlockSpec((None, N_BRANCH, BRANCH_W, d), lambda i: (layer, 0, 0, 0)),
                  pl.BlockSpec((None, d, d), lambda i: (layer, 0, 0))],
        out_specs=rowspec(d),
        out_shape=jax.ShapeDtypeStruct((rows, d), F32),
        compiler_params=_params("parallel"),
        name="merge",
    )(x, ya, yb, yc, z, w_branch, w_out)


def _ffn_kernel(x_ref, g_ref, wu_ref, wd_ref, o_ref, hf_s, acc_s):
    j = pl.program_id(1)

    @pl.when(j == 0)
    def _():
        hf_s[...] = _rms_rows(x_ref[...], g_ref[...]).astype(BF16)
        acc_s[...] = jnp.zeros_like(acc_s)

    up = jnp.dot(hf_s[...], wu_ref[...], preferred_element_type=F32)
    act = jnp.square(jnp.maximum(up, 0.0))
    acc_s[...] += jnp.dot(act.astype(BF16), wd_ref[...], preferred_element_type=F32)

    @pl.when(j == pl.num_programs(1) - 1)
    def _():
        o_ref[...] = x_ref[...] + acc_s[...]


def _ffn(x, g, w_up, w_down, layer, *, tm, tf):
    rows, d = x.shape
    dff = w_up.shape[-1]
    return pl.pallas_call(
        _ffn_kernel,
        grid=(rows // tm, dff // tf),
        in_specs=[pl.BlockSpec((tm, d), lambda i, j: (i, 0)),
                  pl.BlockSpec((None, 1, d), lambda i, j: (layer, 0, 0)),
                  pl.BlockSpec((None, d, tf), lambda i, j: (layer, 0, j)),
                  pl.BlockSpec((None, tf, d), lambda i, j: (layer, j, 0))],
        out_specs=pl.BlockSpec((tm, d), lambda i, j: (i, 0)),
        out_shape=jax.ShapeDtypeStruct((rows, d), F32),
        scratch_shapes=[pltpu.VMEM((tm, d), BF16), pltpu.VMEM((tm, d), F32)],
        compiler_params=_params("parallel", "arbitrary"),
        name="ffn",
    )(x, g, w_up, w_down)


def _final_kernel(x_ref, g_ref, o_ref, *, r0, n_out):
    o_ref[...] = _rms_rows(x_ref[r0:r0 + n_out, :], g_ref[...])


def _final_norm(x3, g, *, r0, n_out):
    nb, tp, d = x3.shape
    return pl.pallas_call(
        functools.partial(_final_kernel, r0=r0, n_out=n_out),
        grid=(nb,),
        in_specs=[pl.BlockSpec((None, tp, d), lambda b: (b, 0, 0)), pl.BlockSpec((1, d), lambda b: (0, 0))],
        out_specs=pl.BlockSpec((None, n_out, d), lambda b: (b, 0, 0)),
        out_shape=jax.ShapeDtypeStruct((nb, n_out, d), F32),
        compiler_params=_params("parallel"),
        name="final_norm",
    )(x3, g)


def _pick_tile(n, pref):
    t = pref
    while n % t:
        t //= 2
    return t


def _layer(x, p, layer, st, *, n_batch, tp, t_valid, prompt):
    d = x.shape[1]
    rows = n_batch * tp
    off = p["off"]
    tm = _pick_tile(rows, 512)
    z = _inproj(x, p["norm_mix"], p["w_in"], layer, tm=tm, n_col_tiles=p["n_col_tiles"])
    k, v, ki = _kvprep(z, off, p["k_norm"], p["idx_k_norm"], layer, tm=tm)
    k3, v3, ki3 = (a.reshape(n_batch, tp, a.shape[-1]) for a in (k, v, ki))
    k_new, v_new, ki_new = k3[:, :t_valid], v3[:, :t_valid], ki3[:, :t_valid]

    if prompt:
        kmat, vmat, kimat = k3, v3, ki3
        l_valid = t_valid
        top_k = min(TOP_K_MAX, (t_valid - N_META) // 4)
        tq, n_q = ROW_PAD, tp // ROW_PAD
    else:
        past = st[0].shape[1]
        l_valid = past + t_valid
        lp = -(-l_valid // LANES) * LANES
        cat = lambda old, new: jnp.concatenate(
            [old.reshape(n_batch, past, -1).astype(F32), new,
             jnp.zeros((n_batch, lp - l_valid, new.shape[-1]), F32)], axis=1)
        kmat, vmat, kimat = cat(st[0], k_new), cat(st[1], v_new), cat(st[2], ki_new)
        top_k = min(TOP_K_MAX, l_valid // 4)
        tq, n_q = t_valid, 1
    ya = _dsa(z, off, kmat, vmat, kimat, p["q_norm"], layer, n_batch=n_batch, rows_per_batch=tp, tq=tq,
              n_q=n_q, l_valid=l_valid, top_k=top_k, prompt=prompt)
    if n_q * tq < tp:
        ya = jnp.pad(ya.reshape(n_batch, n_q * tq, ATT_W), ((0, 0), (0, tp - n_q * tq), (0, 0))).reshape(rows, ATT_W)

    if prompt:
        c0 = jnp.zeros((n_batch, M_HEADS, M_V, M_QK), F32)
        n0 = jnp.zeros((n_batch, M_HEADS, 1, M_QK), F32)
        m0 = jnp.zeros((n_batch, 1, LANES), F32)
        h0 = jnp.zeros((n_batch, 1, LRU_W), F32)
        buf0 = jnp.zeros((n_batch, 8, LRU_W), F32)
    else:
        c0 = st[3].astype(F32)
        n0 = st[4].astype(F32).reshape(n_batch, M_HEADS, 1, M_QK)
        m0 = jnp.pad(st[5].astype(F32), ((0, 0), (0, LANES - M_HEADS))).reshape(n_batch, 1, LANES)
        h0 = st[6].astype(F32).reshape(n_batch, 1, LRU_W)
        buf0 = jnp.pad(st[7].astype(F32), ((0, 0), (8 - (CONV_W - 1), 0), (0, 0)))
    yb, c1, n1, m1 = _mlstm(z, off, p["m_bias_i"], p["m_bias_f"], p["m_norm"], c0, n0, m0, layer,
                            n_batch=n_batch, tp=tp, t_valid=t_valid, lc=ROW_PAD)
    yc, h1, buf1 = _lru(z, off, p["conv_w"], p["conv_b"], p["lru_wa"], p["lru_ba"], p["lru_wx"], p["lru_bx"],
                        p["lru_lambda"], h0, buf0, layer, n_batch=n_batch, tp=tp, t_valid=t_valid, tt=ROW_PAD)

    x = _merge(x, ya, yb, yc, z, off, p["w_branch"], p["w_out"], layer, tm=tm)
    x = _ffn(x, p["norm_ffn"], p["w_up"], p["w_down"], layer, tm=tm, tf=_pick_tile(p["w_up"].shape[-1], 2048))
    state = (k_new.reshape(n_batch, t_valid, N_KV_HEADS, HEAD_DIM),
             v_new.reshape(n_batch, t_valid, N_KV_HEADS, HEAD_DIM), ki_new,
             c1, n1.reshape(n_batch, M_HEADS, M_QK), m1[:, 0, :M_HEADS],
             h1.reshape(n_batch, LRU_W), buf1[:, 8 - (CONV_W - 1):])
    return x, state


def _block_diag(w):
    depth, nb, c, _ = w.shape
    eye = jnp.eye(nb, dtype=w.dtype)
    return jnp.einsum("lncd,nm->lncmd", w, eye).reshape(depth, nb * c, nb * c)


def kernel(x_prompt, x_sample, cache_k, cache_v, cache_idx_k, state_mlstm_c, state_mlstm_n, state_mlstm_m, state_lru_h, state_conv, meta_tokens, norm_mix, w_in, q_norm, k_norm, idx_k_norm, m_bias_i, m_bias_f, m_norm, conv_w, conv_b, lru_wa, lru_ba, lru_wx, lru_bx, lru_lambda, w_branch, w_out, norm_ffn, w_up, w_down, final_norm):
    depth = w_in.shape[0]
    nb, seq, d = x_prompt.shape
    nbs, dseq, _ = x_sample.shape
    n_col_tiles = 3
    w_packed, off, _ = _pack_w_in(w_in, d, n_col_tiles)
    vec3 = lambda a: a.astype(F32).reshape(depth, 1, -1)
    p = dict(off=off, n_col_tiles=n_col_tiles, w_in=w_packed,
             norm_mix=vec3(norm_mix), q_norm=vec3(q_norm), k_norm=vec3(k_norm), idx_k_norm=vec3(idx_k_norm),
             m_bias_i=m_bias_i.astype(F32), m_bias_f=m_bias_f.astype(F32), m_norm=vec3(m_norm),
             conv_w=conv_w.astype(F32), conv_b=vec3(conv_b),
             lru_wa=_block_diag(lru_wa.astype(F32)), lru_ba=vec3(lru_ba),
             lru_wx=_block_diag(lru_wx.astype(F32)), lru_bx=vec3(lru_bx), lru_lambda=vec3(lru_lambda),
             w_branch=w_branch.astype(BF16), w_out=w_out.astype(BF16), norm_ffn=vec3(norm_ffn),
             w_up=w_up.astype(BF16), w_down=w_down.astype(BF16))

    t_p = N_META + seq
    tp_p = -(-t_p // ROW_PAD) * ROW_PAD
    meta = jnp.broadcast_to(meta_tokens[None].astype(x_prompt.dtype), (nb, N_META, d))
    hp = jnp.concatenate([meta, x_prompt, jnp.zeros((nb, tp_p - t_p, d), x_prompt.dtype)], axis=1)
    hp = hp.reshape(nb * tp_p, d)
    tp_s = -(-dseq // ROW_PAD) * ROW_PAD
    hs = jnp.pad(x_sample, ((0, 0), (0, tp_s - dseq), (0, 0))).reshape(nbs * tp_s, d)

    p_states, s_states = [], []
    for l in range(depth):
        hp, stp = _layer(hp, p, l, None, n_batch=nb, tp=tp_p, t_valid=t_p, prompt=True)
        hs, sts = _layer(hs, p, l, (cache_k[l], cache_v[l], cache_idx_k[l], state_mlstm_c[l], state_mlstm_n[l],
                                    state_mlstm_m[l], state_lru_h[l], state_conv[l]),
                         n_batch=nbs, tp=tp_s, t_valid=dseq, prompt=False)
        p_states.append(stp)
        s_states.append(sts)
    fg = final_norm.astype(F32).reshape(1, d)
    y_prompt = _final_norm(hp.reshape(nb, tp_p, d), fg, r0=N_META, n_out=seq)
    y_sample = _final_norm(hs.reshape(nbs, tp_s, d), fg, r0=0, n_out=dseq)
    pouts = [jnp.stack(t) for t in zip(*p_states)]
    souts = [jnp.stack(t) for t in zip(*s_states)]
    return (y_prompt, y_sample, *pouts, *souts)
```

```python
import functools

import numpy as np
import jax
import jax.numpy as jnp
from jax import lax
from jax.experimental import pallas as pl
from jax.experimental.pallas import tpu as pltpu

F32 = jnp.float32
BF16 = jnp.bfloat16
I32 = jnp.int32

CHUNK = 64
N_META = 16
N_HEADS = 8
N_KV_HEADS = 2
HEAD_DIM = 64
N_IDX_HEADS = 4
IDX_DIM = 64
TOP_K_MAX = 256
M_HEADS = 4
M_QK = 64
M_V = 128
LRU_W = 512
LRU_BLOCKS = 8
CONV_W = 4
LRU_C = 8.0
N_BRANCH = 3
BRANCH_W = 512
EPS = 1e-6
NEG = -1e30

ATT_W = N_HEADS * HEAD_DIM
KV_W = N_KV_HEADS * HEAD_DIM
GROUP = N_HEADS // N_KV_HEADS

LANES = 128
ROW_PAD = 128
VMEM_LIMIT = 56 * 1024 * 1024
DSA_TILES_PER_GROUP = 2
MLSTM_SEQS = 1

SM_IK = 0
SM_IW = IDX_DIM
SM_MI = SM_IW + N_IDX_HEADS
SM_MF = SM_MI + M_HEADS
SM_W = 128


def _z_layout(d_model):
    widths = [("gt", N_BRANCH * d_model), ("aq", ATT_W), ("mv", M_HEADS * M_V), ("mo", M_HEADS * M_V),
              ("lx", LRU_W), ("lg", LRU_W), ("iq", N_IDX_HEADS * IDX_DIM), ("mq", M_HEADS * M_QK),
              ("mk", M_HEADS * M_QK), ("ak", KV_W), ("av", KV_W), ("sm", SM_W)]
    off, acc = {}, 0
    for name, w in widths:
        assert acc % w == 0, (name, acc, w)
        off[name] = (acc, w)
        acc += w
    return off, acc


def _pack_w_in(w_in, d_model, n_col_tiles):
    sizes = (ATT_W, KV_W, KV_W, N_IDX_HEADS * IDX_DIM, IDX_DIM, N_IDX_HEADS,
             M_HEADS * M_QK, M_HEADS * M_QK, M_HEADS * M_V, M_HEADS, M_HEADS, M_HEADS * M_V,
             LRU_W, LRU_W, N_BRANCH * d_model)
    names = ("aq", "ak", "av", "iq", "ik", "iw", "mq", "mk", "mv", "mi", "mf", "mo", "lx", "lg", "gt")
    assert w_in.shape[-1] == sum(sizes)
    parts, acc = {}, 0
    for n, s in zip(names, sizes):
        parts[n] = w_in[..., acc:acc + s]
        acc += s
    lead = w_in.shape[:-1]
    small = jnp.concatenate([parts["ik"], parts["iw"], parts["mi"], parts["mf"],
                             jnp.zeros(lead + (SM_W - SM_MF - M_HEADS,), w_in.dtype)], axis=-1)
    parts["sm"] = small
    off, zw = _z_layout(d_model)
    order = sorted(off, key=lambda n: off[n][0])
    cols = [parts[n] for n in order]
    zw_pad = -(-zw // (n_col_tiles * LANES)) * (n_col_tiles * LANES)
    if zw_pad > zw:
        cols.append(jnp.zeros(lead + (zw_pad - zw,), w_in.dtype))
    return jnp.concatenate(cols, axis=-1).astype(BF16), off, zw_pad


def _params(*sem):
    return pltpu.CompilerParams(dimension_semantics=sem, vmem_limit_bytes=VMEM_LIMIT)


def _bdot(a, b):
    return jnp.dot(a.astype(BF16), b.astype(BF16), preferred_element_type=F32)


def _bdot_nt(a, b):
    return lax.dot_general(a.astype(BF16), b.astype(BF16), (((1,), (1,)), ((), ())),
                           preferred_element_type=F32)


def _split3(a):
    a1 = a.astype(BF16)
    r1 = a - a1.astype(F32)
    a2 = r1.astype(BF16)
    a3 = (r1 - a2.astype(F32)).astype(BF16)
    return a1, a2, a3


def _dot3_nt(a, b):
    a1, a2, _ = _split3(a)
    b1, b2, _ = _split3(b)
    dn = (((1,), (1,)), ((), ()))
    f = lambda x, y: lax.dot_general(x, y, dn, preferred_element_type=F32)
    return f(a1, b1) + (f(a1, b2) + f(a2, b1))


def _dot3(a, b):
    a1, a2, _ = _split3(a)
    b1, b2, _ = _split3(b)
    f = lambda x, y: jnp.dot(x, y, preferred_element_type=F32)
    return f(a1, b1) + (f(a1, b2) + f(a2, b1))


def _rms_rows(x, g):
    return x * lax.rsqrt(jnp.mean(x * x, axis=-1, keepdims=True) + EPS) * g


def _sigmoid(x):
    return 1.0 / (1.0 + jnp.exp(-x))


def _log_sigmoid(x):
    return jnp.minimum(x, 0.0) - jnp.log1p(jnp.exp(-jnp.abs(x)))


def _softplus(x):
    return jnp.maximum(x, 0.0) + jnp.log1p(jnp.exp(-jnp.abs(x)))


def _gelu_tanh(x):
    c = np.float32(np.sqrt(2.0 / np.pi))
    return 0.5 * x * (1.0 + jnp.tanh(c * (x + np.float32(0.044715) * (x * x * x))))


def _inproj_kernel(x_ref, g_ref, w_ref, z_ref):
    hn = _rms_rows(x_ref[...], g_ref[...])
    z_ref[...] = jnp.dot(hn.astype(BF16), w_ref[...], preferred_element_type=F32)


def _inproj(x, g, w_packed, layer, *, tm, n_col_tiles):
    rows, d = x.shape
    zw = w_packed.shape[-1]
    tn = zw // n_col_tiles
    return pl.pallas_call(
        _inproj_kernel,
        grid=(n_col_tiles, rows // tm),
        in_specs=[pl.BlockSpec((tm, d), lambda j, i: (i, 0)),
                  pl.BlockSpec((None, 1, d), lambda j, i: (layer, 0, 0)),
                  pl.BlockSpec((None, d, tn), lambda j, i: (layer, 0, j))],
        out_specs=pl.BlockSpec((tm, tn), lambda j, i: (i, j)),
        out_shape=jax.ShapeDtypeStruct((rows, zw), F32),
        compiler_params=_params("parallel", "parallel"),
        name="inproj",
    )(x, g, w_packed)


def _kvprep_kernel(ak_ref, av_ref, sm_ref, kn_ref, ikn_ref, k_ref, v_ref, ki_ref, kb_ref, vt_ref, kic_ref):
    ak = ak_ref[...]
    kn = kn_ref[...]
    ks = [_rms_rows(ak[:, h * HEAD_DIM:(h + 1) * HEAD_DIM], kn) for h in range(N_KV_HEADS)]
    k_ref[...] = jnp.concatenate(ks, axis=-1)
    av = av_ref[...]
    v_ref[...] = av
    ki = _rms_rows(sm_ref[...][:, SM_IK:SM_IK + IDX_DIM], ikn_ref[...])
    ki_ref[...] = ki
    for g in range(N_KV_HEADS):
        kb_ref[g] = ks[g].astype(BF16)
    vt_ref[...] = av.T.astype(BF16)
    hi = ki.astype(BF16).astype(F32)
    kic_ref[...] = jnp.concatenate([hi, ki - hi, hi, jnp.zeros_like(hi)], axis=-1).astype(BF16)


def _kvprep(z, off, k_norm, idx_k_norm, layer, *, n_batch, tp):
    rows = z.shape[0]
    cb = lambda name: off[name][0] // off[name][1]
    return pl.pallas_call(
        _kvprep_kernel,
        grid=(n_batch,),
        in_specs=[pl.BlockSpec((tp, KV_W), lambda b: (b, cb("ak"))),
                  pl.BlockSpec((tp, KV_W), lambda b: (b, cb("av"))),
                  pl.BlockSpec((tp, SM_W), lambda b: (b, cb("sm"))),
                  pl.BlockSpec((None, 1, HEAD_DIM), lambda b: (layer, 0, 0)),
                  pl.BlockSpec((None, 1, IDX_DIM), lambda b: (layer, 0, 0))],
        out_specs=[pl.BlockSpec((tp, KV_W), lambda b: (b, 0)),
                   pl.BlockSpec((tp, KV_W), lambda b: (b, 0)),
                   pl.BlockSpec((tp, IDX_DIM), lambda b: (b, 0)),
                   pl.BlockSpec((None, N_KV_HEADS, tp, HEAD_DIM), lambda b: (b, 0, 0, 0)),
                   pl.BlockSpec((None, KV_W, tp), lambda b: (b, 0, 0)),
                   pl.BlockSpec((None, tp, 4 * IDX_DIM), lambda b: (b, 0, 0))],
        out_shape=[jax.ShapeDtypeStruct((rows, KV_W), F32),
                   jax.ShapeDtypeStruct((rows, KV_W), F32),
                   jax.ShapeDtypeStruct((rows, IDX_DIM), F32),
                   jax.ShapeDtypeStruct((n_batch, N_KV_HEADS, tp, HEAD_DIM), BF16),
                   jax.ShapeDtypeStruct((n_batch, KV_W, tp), BF16),
                   jax.ShapeDtypeStruct((n_batch, tp, 4 * IDX_DIM), BF16)],
        compiler_params=_params("parallel"),
        name="kvprep",
    )(z, z, z, k_norm, idx_k_norm)


_INT_MIN = np.int32(-2 ** 31)


def _sort_key_of(x):
    b = np.array(x, np.float32).view(np.int32)
    return np.int32(b ^ 0x7FFFFFFF) if b < 0 else np.int32(b)


_OK_KEY = _sort_key_of(0.5 * NEG)


def _dsa_kernel(aq_ref, iq_ref, sm_ref, k_ref, v_ref, ki_ref, qn_ref, o_ref, key_ref, msk_ref,
                *, tq, lp, l_valid, top_k, prompt):
    nblk = lp // LANES
    kf = np.float32(top_k)

    iq = iq_ref[...]
    sm = sm_ref[...]
    ki = ki_ref[...]
    sc = None
    for h in range(N_IDX_HEADS):
        s = _dot3_nt(iq[:, h * IDX_DIM:(h + 1) * IDX_DIM], ki)
        s = jnp.maximum(s * np.float32(IDX_DIM ** -0.5), 0.0)
        w = sm[:, SM_IW + h:SM_IW + h + 1] * np.float32(N_IDX_HEADS ** -0.5)
        sc = s * w if sc is None else sc + s * w
    sc = jnp.where(sc == 0.0, 0.0, sc)

    kpos = lax.broadcasted_iota(I32, (1, lp), 1)
    if prompt:
        qpos = pl.program_id(1) * tq + lax.broadcasted_iota(I32, (tq, 1), 0)
        allowed = ((kpos + (CHUNK - N_META)) // CHUNK <= (qpos + (CHUNK - N_META)) // CHUNK) & (kpos < l_valid)
    else:
        allowed = jnp.broadcast_to(kpos < l_valid, (tq, lp))
    scm = jnp.where(allowed, sc, np.float32(NEG))
    bits = pltpu.bitcast(scm, I32)
    key_ref[...] = jnp.where(bits < 0, bits ^ np.int32(0x7FFFFFFF), bits)

    def count_ge(cand):
        acc = jnp.zeros((tq, LANES), F32)
        for j in range(nblk):
            acc = acc + jnp.where(key_ref[:, j * LANES:(j + 1) * LANES] >= cand, 1.0, 0.0)
        return jnp.sum(acc, axis=-1, keepdims=True)

    ans = jnp.where(count_ge(jnp.zeros((tq, 1), I32)) >= kf, np.int32(0), _INT_MIN)

    def bit_step(t, ans):
        cand = ans | lax.shift_left(np.int32(1), np.int32(30) - t)
        return jnp.where(count_ge(cand) >= kf, cand, ans)

    thr = lax.fori_loop(0, 31, bit_step, ans)

    acc = jnp.zeros((tq, LANES), F32)
    for j in range(nblk):
        acc = acc + jnp.where(key_ref[:, j * LANES:(j + 1) * LANES] > thr, 1.0, 0.0)
    need = kf - jnp.sum(acc, axis=-1, keepdims=True)
    tri = (lax.broadcasted_iota(I32, (LANES, LANES), 0) < lax.broadcasted_iota(I32, (LANES, LANES), 1))
    tri = jnp.where(tri, 1.0, 0.0).astype(BF16)
    seen = jnp.zeros((tq, 1), F32)
    for j in range(nblk):
        kj = key_ref[:, j * LANES:(j + 1) * LANES]
        eq = jnp.where(kj == thr, 1.0, 0.0)
        before = seen + jnp.dot(eq.astype(BF16), tri, preferred_element_type=F32)
        sel = (kj > thr) | ((kj == thr) & (before < need))
        sel = sel & (kj > _OK_KEY)
        msk_ref[:, j * LANES:(j + 1) * LANES] = jnp.where(sel, 1.0, 0.0)
        seen = seen + jnp.sum(eq, axis=-1, keepdims=True)

    q = aq_ref[...]
    qn = qn_ref[...]
    kk = k_ref[...]
    vv = v_ref[...]
    sel = msk_ref[...] > 0.5
    outs = []
    for g in range(N_KV_HEADS):
        kg = kk[:, g * HEAD_DIM:(g + 1) * HEAD_DIM].astype(BF16)
        vg = vv[:, g * HEAD_DIM:(g + 1) * HEAD_DIM].astype(BF16)
        for hh in range(GROUP):
            h = g * GROUP + hh
            qh = _rms_rows(q[:, h * HEAD_DIM:(h + 1) * HEAD_DIM], qn)
            s = _bdot_nt(qh, kg) * np.float32(HEAD_DIM ** -0.5)
            s = jnp.where(sel, s, np.float32(NEG))
            p = jnp.exp(s - jnp.max(s, axis=-1, keepdims=True))
            den = jnp.sum(p, axis=-1, keepdims=True)
            outs.append(jnp.dot(p.astype(BF16), vg, preferred_element_type=F32) / den)
    o_ref[...] = jnp.concatenate(outs, axis=-1)


def _dsa(z, off, kmat, vmat, kimat, q_norm, layer, *, n_batch, rows_per_batch, tq, n_q, l_valid, top_k, prompt):
    lp = kmat.shape[1]
    assert lp % LANES == 0 and rows_per_batch % tq == 0
    rb = rows_per_batch // tq
    cb = lambda name: off[name][0] // off[name][1]
    kern = functools.partial(_dsa_kernel, tq=tq, lp=lp, l_valid=l_valid, top_k=top_k, prompt=prompt)
    return pl.pallas_call(
        kern,
        grid=(n_batch, n_q),
        in_specs=[pl.BlockSpec((tq, ATT_W), lambda b, i: (b * rb + i, cb("aq"))),
                  pl.BlockSpec((tq, N_IDX_HEADS * IDX_DIM), lambda b, i: (b * rb + i, cb("iq"))),
                  pl.BlockSpec((tq, SM_W), lambda b, i: (b * rb + i, cb("sm"))),
                  pl.BlockSpec((None, lp, KV_W), lambda b, i: (b, 0, 0)),
                  pl.BlockSpec((None, lp, KV_W), lambda b, i: (b, 0, 0)),
                  pl.BlockSpec((None, lp, IDX_DIM), lambda b, i: (b, 0, 0)),
                  pl.BlockSpec((None, 1, HEAD_DIM), lambda b, i: (layer, 0, 0))],
        out_specs=pl.BlockSpec((tq, ATT_W), lambda b, i: (b * n_q + i, 0)),
        out_shape=jax.ShapeDtypeStruct((n_batch * n_q * tq, ATT_W), F32),
        scratch_shapes=[pltpu.VMEM((tq, lp), I32), pltpu.VMEM((tq, lp), F32)],
        compiler_params=_params("parallel", "parallel"),
        name="dsa",
    )(z, z, z, kmat, vmat, kimat, q_norm)


def _row_blocks(x):
    return [x[j * LANES:(j + 1) * LANES, :] for j in range(x.shape[0] // LANES)]


def _col_reduce(blocks, op, red):
    a = blocks[0]
    for blk in blocks[1:]:
        a = op(a, blk)
    r = a.shape[0]
    while r > 8:
        r //= 2
        a = op(a[:r], a[r:])
    return red(a, axis=0, keepdims=True)


def _order_key(bits):
    return jnp.where(bits < 0, _INT_MIN - bits, bits)


_OK_KEY_T = np.int32(-int(np.array(-0.5 * NEG, np.float32).view(np.int32)))
_BITS_PER_ROUND = 4


def _dsa_t_kernel(aq_ref, iq_ref, sm_ref, kb_ref, vt_ref, kic_ref, qn_ref, o_ref, key_ref, bias_ref,
                  *, tq, lp, l_valid, top_k, q0):
    kf = np.float32(top_k)
    nt = (((1,), (1,)), ((), ()))

    smt = sm_ref[...].T
    iq = iq_ref[...]
    pieces = []
    for h in range(N_IDX_HEADS):
        qh = iq[:, h * IDX_DIM:(h + 1) * IDX_DIM]
        hi = qh.astype(BF16).astype(F32)
        pieces.append(jnp.concatenate([hi, hi, qh - hi, jnp.zeros_like(hi)], axis=-1))
    qcat = jnp.concatenate(pieces, axis=0).astype(BF16)
    st = lax.dot_general(kic_ref[...], qcat, nt, preferred_element_type=F32)
    sc = None
    for h in range(N_IDX_HEADS):
        w = smt[SM_IW + h:SM_IW + h + 1, :] * np.float32(N_IDX_HEADS ** -0.5 * IDX_DIM ** -0.5)
        t = jnp.maximum(st[:, h * tq:(h + 1) * tq], 0.0) * w
        sc = t if sc is None else sc + t

    kpos = lax.broadcasted_iota(I32, (lp, 1), 0)
    qpos = (q0 + pl.program_id(1)) * tq + lax.broadcasted_iota(I32, (1, tq), 1)
    allowed = ((kpos + (CHUNK - N_META)) // CHUNK <= (qpos + (CHUNK - N_META)) // CHUNK) & (kpos < l_valid)
    scm = jnp.where(allowed, sc, np.float32(NEG))
    key_ref[...] = _order_key(pltpu.bitcast(scm, I32))

    def count_ge(cand):
        return _col_reduce([jnp.where(key_ref[j * LANES:(j + 1) * LANES, :] >= cand, 1.0, 0.0)
                            for j in range(lp // LANES)], jnp.add, jnp.sum)

    def rounds_cond(c):
        return (c[0] < 32 // _BITS_PER_ROUND) & (c[3] == 0)

    def rounds_body(c):
        r, ans, cnt, _ = c
        for u in range(_BITS_PER_ROUND):
            bit = np.int32(31) - (r * _BITS_PER_ROUND + u)
            cand = ans + lax.shift_left(np.int32(1), bit)
            cc = count_ge(cand)
            take = cc >= kf
            ans = jnp.where(take, cand, ans)
            cnt = jnp.where(take, cc, cnt)
        done = jnp.min(jnp.where(cnt == kf, 1.0, 0.0)) > 0.5
        return r + 1, ans, cnt, done.astype(I32)

    init = (np.int32(0), jnp.full((1, tq), _INT_MIN, I32), jnp.full((1, tq), np.float32(lp), F32), np.int32(0))
    _, thr, _, exact = lax.while_loop(rounds_cond, rounds_body, init)

    @pl.when(exact == 1)
    def _():
        key = key_ref[...]
        bias_ref[...] = jnp.where((key >= thr) & (key > _OK_KEY_T), 0.0, np.float32(NEG))

    @pl.when(exact == 0)
    def _():
        need = kf - _col_reduce([jnp.where(key_ref[j * LANES:(j + 1) * LANES, :] > thr, 1.0, 0.0)
                                 for j in range(lp // LANES)], jnp.add, jnp.sum)
        tri = (lax.broadcasted_iota(I32, (LANES, LANES), 1) < lax.broadcasted_iota(I32, (LANES, LANES), 0))
        tri = jnp.where(tri, 1.0, 0.0).astype(BF16)
        seen = jnp.zeros((1, tq), F32)
        for j in range(lp // LANES):
            kj = key_ref[j * LANES:(j + 1) * LANES, :]
            eq = jnp.where(kj == thr, 1.0, 0.0)
            before = seen + jnp.dot(tri, eq.astype(BF16), preferred_element_type=F32)
            sel = ((kj > thr) | ((kj == thr) & (before < need))) & (kj > _OK_KEY_T)
            bias_ref[j * LANES:(j + 1) * LANES, :] = jnp.where(sel, 0.0, np.float32(NEG))
            seen = seen + jnp.sum(eq, axis=0, keepdims=True)

    q = aq_ref[...]
    qn = qn_ref[...] * np.float32(HEAD_DIM ** -0.5)
    bias = bias_ref[...]
    outs = []
    for g in range(N_KV_HEADS):
        qs = [_rms_rows(q[:, h * HEAD_DIM:(h + 1) * HEAD_DIM], qn) for h in range(g * GROUP, (g + 1) * GROUP)]
        qs = jnp.concatenate(qs, axis=0).astype(BF16)
        s_all = lax.dot_general(kb_ref[g], qs, nt, preferred_element_type=F32)
        vtg = vt_ref[g * HEAD_DIM:(g + 1) * HEAD_DIM, :]
        for hh in range(GROUP):
            s = s_all[:, hh * tq:(hh + 1) * tq] + bias
            p = jnp.exp(s - _col_reduce(_row_blocks(s), jnp.maximum, jnp.max))
            den = _col_reduce(_row_blocks(p), jnp.add, jnp.sum)
            outs.append(jnp.dot(vtg, p.astype(BF16), preferred_element_type=F32) / den)
    o_ref[...] = jnp.concatenate(outs, axis=0).T


def _dsa_t(z, off, kb, vt, kic, q_norm, layer, *, n_batch, tp, tq, q0, n_q, lp, l_valid, top_k):
    rb = tp // tq
    cb = lambda name: off[name][0] // off[name][1]
    kern = functools.partial(_dsa_t_kernel, tq=tq, lp=lp, l_valid=l_valid, top_k=top_k, q0=q0)
    qrow = lambda b, i: b * rb + q0 + i
    return pl.pallas_call(
        kern,
        grid=(n_batch, n_q),
        in_specs=[pl.BlockSpec((tq, ATT_W), lambda b, i: (qrow(b, i), cb("aq"))),
                  pl.BlockSpec((tq, N_IDX_HEADS * IDX_DIM), lambda b, i: (qrow(b, i), cb("iq"))),
                  pl.BlockSpec((tq, SM_W), lambda b, i: (qrow(b, i), cb("sm"))),
                  pl.BlockSpec((None, N_KV_HEADS, lp, HEAD_DIM), lambda b, i: (b, 0, 0, 0)),
                  pl.BlockSpec((None, KV_W, lp), lambda b, i: (b, 0, 0)),
                  pl.BlockSpec((None, lp, 4 * IDX_DIM), lambda b, i: (b, 0, 0)),
                  pl.BlockSpec((None, 1, HEAD_DIM), lambda b, i: (layer, 0, 0))],
        out_specs=pl.BlockSpec((None, tq, ATT_W), lambda b, i: (b, i, 0)),
        out_shape=jax.ShapeDtypeStruct((n_batch, n_q * tq, ATT_W), F32),
        scratch_shapes=[pltpu.VMEM((lp, tq), I32), pltpu.VMEM((lp, tq), F32)],
        compiler_params=_params("parallel", "parallel"),
        name="dsa_t",
    )(z, z, z, kb, vt, kic, q_norm)


def _mlstm_kernel(bi_ref, bf_ref, mq_ref, mk_ref, mv_ref, mo_ref, sm_ref, gn_ref, c0_ref, n0_ref, m0_ref,
                  yb_ref, c1_ref, n1_ref, m1_ref, c_s, n_s, m_s, *, lc, t_valid, layer, nbb):
    j = pl.program_id(1)

    @pl.when(j == 0)
    def _():
        c_s[...] = c0_ref[...]
        n_s[...] = n0_ref[...]
        m_s[...] = m0_ref[...]

    t_io = lax.broadcasted_iota(I32, (lc, lc), 0)
    s_io = lax.broadcasted_iota(I32, (lc, lc), 1)
    causal = s_io <= t_io
    pos_col = j * lc + lax.broadcasted_iota(I32, (lc, 1), 0)
    pos_row = j * lc + lax.broadcasted_iota(I32, (1, lc), 1)
    lane = lax.broadcasted_iota(I32, (1, LANES), 1)
    gn = gn_ref[...]
    c_in = [[c_s[bb, h] for h in range(M_HEADS)] for bb in range(nbb)]
    n_in = [[n_s[bb, h] for h in range(M_HEADS)] for bb in range(nbb)]
    m_in = [m_s[bb] for bb in range(nbb)]
    c_out, n_out, m_outs, y_out = [], [], [], []
    for bb in range(nbb):
        sm = sm_ref[bb]
        smt = sm.T
        mq = mq_ref[bb]
        mk = mk_ref[bb]
        mv = mv_ref[bb]
        mo = mo_ref[bb]
        m_all = m_in[bb]
        m_out = m_all
        for h in range(M_HEADS):
            bi = bi_ref[layer, h]
            bf = bf_ref[layer, h]
            i_col = jnp.where(pos_col < t_valid, sm[:, SM_MI + h:SM_MI + h + 1] + bi, np.float32(NEG))
            f_col = jnp.where(pos_col < t_valid, _log_sigmoid(sm[:, SM_MF + h:SM_MF + h + 1] + bf), 0.0)
            i_row = jnp.where(pos_row < t_valid, smt[SM_MI + h:SM_MI + h + 1, :] + bi, np.float32(NEG))
            f_row = jnp.where(pos_row < t_valid, _log_sigmoid(smt[SM_MF + h:SM_MF + h + 1, :] + bf), 0.0)
            b_col = jnp.sum(jnp.where(causal, f_row, 0.0), axis=1, keepdims=True)
            b_row = jnp.sum(jnp.where(t_io <= s_io, f_col, 0.0), axis=0, keepdims=True)
            m_prev = m_all[:, h:h + 1]
            log_d = jnp.where(causal, b_col - b_row + i_row, np.float32(NEG))
            m_t = jnp.maximum(b_col + m_prev, jnp.max(log_d, axis=1, keepdims=True))
            d = jnp.exp(log_d - m_t)
            inter = jnp.exp(b_col + m_prev - m_t)
            qh = mq[:, h * M_QK:(h + 1) * M_QK]
            kh = mk[:, h * M_QK:(h + 1) * M_QK] * np.float32(M_QK ** -0.5)
            vh = mv[:, h * M_V:(h + 1) * M_V]
            c_h = c_in[bb][h]
            n_h = n_in[bb][h]
            w = _bdot_nt(qh, kh) * d
            num = _bdot(w, vh) + inter * _bdot_nt(qh, c_h)
            den = jnp.sum(w, axis=1, keepdims=True) + inter * jnp.sum(qh * n_h, axis=1, keepdims=True)
            hm = num / jnp.maximum(jnp.abs(den), jnp.exp(-m_t))
            m_new = m_t[lc - 1:lc, :]
            b_last = b_col[lc - 1:lc, :]
            decay = jnp.exp(b_last + m_prev - m_new)
            ws = jnp.exp(b_last - b_col + i_col - m_new)
            c_out.append(decay * c_h + _bdot((vh * ws).T, kh))
            n_out.append(decay * n_h + jnp.sum(ws * kh, axis=0, keepdims=True))
            m_out = jnp.where(lane == h, m_new, m_out)
            y = _rms_rows(hm, gn[:, h * M_V:(h + 1) * M_V])
            y_out.append(y * _sigmoid(mo[:, h * M_V:(h + 1) * M_V]))
        m_outs.append(m_out)
    for bb in range(nbb):
        for h in range(M_HEADS):
            c_s[bb, h] = c_out[bb * M_HEADS + h]
            n_s[bb, h] = n_out[bb * M_HEADS + h]
            yb_ref[bb, :, h * M_V:(h + 1) * M_V] = y_out[bb * M_HEADS + h]
        m_s[bb] = m_outs[bb]

    @pl.when(j == pl.num_programs(1) - 1)
    def _():
        c1_ref[...] = c_s[...]
        n1_ref[...] = n_s[...]
        m1_ref[...] = m_s[...]


def _mlstm(z, off, bias_i, bias_f, m_norm, c0, n0, m0, layer, *, n_batch, tp, t_valid, lc, nbb):
    assert n_batch % nbb == 0
    nc = tp // lc
    z3 = z.reshape(n_batch, tp, z.shape[-1])
    cb = lambda name: off[name][0] // off[name][1]
    row = lambda name: pl.BlockSpec((nbb, lc, off[name][1]), lambda b, j: (b, j, cb(name)))
    smem = pl.BlockSpec(memory_space=pltpu.SMEM)
    st_c = pl.BlockSpec((nbb, M_HEADS, M_V, M_QK), lambda b, j: (b, 0, 0, 0))
    st_n = pl.BlockSpec((nbb, M_HEADS, 1, M_QK), lambda b, j: (b, 0, 0, 0))
    st_m = pl.BlockSpec((nbb, 1, LANES), lambda b, j: (b, 0, 0))
    kern = functools.partial(_mlstm_kernel, lc=lc, t_valid=t_valid, layer=layer, nbb=nbb)
    yb, c1, n1, m1 = pl.pallas_call(
        kern,
        grid=(n_batch // nbb, nc),
        in_specs=[smem, smem, row("mq"), row("mk"), row("mv"), row("mo"), row("sm"),
                  pl.BlockSpec((None, 1, M_HEADS * M_V), lambda b, j: (layer, 0, 0)),
                  st_c, st_n, st_m],
        out_specs=[pl.BlockSpec((nbb, lc, M_HEADS * M_V), lambda b, j: (b, j, 0)), st_c, st_n, st_m],
        out_shape=[jax.ShapeDtypeStruct((n_batch, tp, M_HEADS * M_V), F32),
                   jax.ShapeDtypeStruct(c0.shape, F32), jax.ShapeDtypeStruct(n0.shape, F32),
                   jax.ShapeDtypeStruct(m0.shape, F32)],
        scratch_shapes=[pltpu.VMEM((nbb, M_HEADS, M_V, M_QK), F32), pltpu.VMEM((nbb, M_HEADS, 1, M_QK), F32),
                        pltpu.VMEM((nbb, 1, LANES), F32)],
        compiler_params=_params("parallel", "arbitrary"),
        name="mlstm",
    )(bias_i, bias_f, z3, z3, z3, z3, z3, m_norm, c0, n0, m0)
    return yb.reshape(n_batch * tp, M_HEADS * M_V), c1, n1, m1


def _shift_rows(x, s, fill):
    rolled = pltpu.roll(x, s, 0)
    row = lax.broadcasted_iota(I32, x.shape, 0)
    return jnp.where(row >= s, rolled, fill)


def _lru_kernel(lx_ref, lg_ref, cw_ref, cb_ref, wa_ref, ba_ref, wx_ref, bx_ref, lam_ref, h0_ref, buf0_ref,
                yc_ref, h1_ref, buf1_ref, prev_s, h_s, *, tt, t_valid):
    j = pl.program_id(1)

    @pl.when(j == 0)
    def _():
        prev_s[...] = buf0_ref[...]
        h_s[...] = h0_ref[...]

    lx = lx_ref[...]
    cw = cw_ref[...]
    prev = prev_s[...]
    row8 = lax.broadcasted_iota(I32, prev.shape, 0)
    xc = cb_ref[...] + lx * cw[CONV_W - 1:CONV_W, :]
    for s in range(1, CONV_W):
        rolled = pltpu.roll(lx, s, 0)
        head = jnp.where(row8 < s, pltpu.roll(prev, s, 0), rolled[:8])
        shifted = jnp.concatenate([head, rolled[8:]], axis=0) if tt > 8 else head
        xc = xc + shifted * cw[CONV_W - 1 - s:CONV_W - s, :]
    prev_s[...] = lx[tt - 8:, :]

    r = _sigmoid(_dot3(xc, wa_ref[...]) + ba_ref[...])
    gi = _sigmoid(_dot3(xc, wx_ref[...]) + bx_ref[...])
    log_a = np.float32(-LRU_C) * r * _softplus(-lam_ref[...])
    a = jnp.exp(log_a)
    u = jnp.sqrt(jnp.tanh(-log_a) * (a * a + 1.0)) * gi * xc
    s = 1
    while s < tt:
        u = a * _shift_rows(u, s, 0.0) + u
        a = a * _shift_rows(a, s, 1.0)
        s *= 2
    h = a * h_s[...] + u
    h_s[...] = h[tt - 1:tt, :]
    yc_ref[...] = h * _gelu_tanh(lg_ref[...])

    last = t_valid - 1

    @pl.when(j == last // tt)
    def _():
        r0 = last % tt
        h1_ref[...] = h[r0:r0 + 1, :]
        buf1_ref[...] = lx[r0 - 7:r0 + 1, :]


def _lru(z, off, conv_w, conv_b, wa_bd, ba, wx_bd, bx, lam, h0, buf0, layer, *, n_batch, tp, t_valid, tt):
    assert t_valid % 8 == 0 and tt % 8 == 0
    nt = tp // tt
    cb = lambda name: off[name][0] // off[name][1]
    row = lambda name: pl.BlockSpec((tt, LRU_W), lambda b, j: (b * nt + j, cb(name)))
    vec = lambda n: pl.BlockSpec((None, n, LRU_W), lambda b, j: (layer, 0, 0))
    st_h = pl.BlockSpec((None, 1, LRU_W), lambda b, j: (b, 0, 0))
    st_b = pl.BlockSpec((None, 8, LRU_W), lambda b, j: (b, 0, 0))
    kern = functools.partial(_lru_kernel, tt=tt, t_valid=t_valid)
    return pl.pallas_call(
        kern,
        grid=(n_batch, nt),
        in_specs=[row("lx"), row("lg"), vec(CONV_W), vec(1), vec(LRU_W), vec(1), vec(LRU_W), vec(1), vec(1),
                  st_h, st_b],
        out_specs=[pl.BlockSpec((tt, LRU_W), lambda b, j: (b * nt + j, 0)), st_h, st_b],
        out_shape=[jax.ShapeDtypeStruct((n_batch * tp, LRU_W), F32),
                   jax.ShapeDtypeStruct(h0.shape, F32), jax.ShapeDtypeStruct(buf0.shape, F32)],
        scratch_shapes=[pltpu.VMEM((8, LRU_W), F32), pltpu.VMEM((1, LRU_W), F32)],
        compiler_params=_params("parallel", "arbitrary"),
        name="lru",
    )(z, z, conv_w, conv_b, wa_bd, ba, wx_bd, bx, lam, h0, buf0)


def _merge_kernel(x_ref, ya_ref, yb_ref, yc_ref, gt_ref, wb_ref, wo_ref, o_ref, *, d):
    acc = None
    for b, y_ref in enumerate((ya_ref, yb_ref, yc_ref)):
        proj = jnp.dot(y_ref[...].astype(BF16), wb_ref[b], preferred_element_type=F32)
        t = _sigmoid(gt_ref[:, b * d:(b + 1) * d]) * proj
        acc = t if acc is None else acc + t
    o_ref[...] = x_ref[...] + jnp.dot(acc.astype(BF16), wo_ref[...], preferred_element_type=F32)


def _merge(x, ya, yb, yc, z, off, w_branch, w_out, layer, *, tm):
    rows, d = x.shape
    assert off["gt"][0] == 0
    rowspec = lambda w: pl.BlockSpec((tm, w), lambda i: (i, 0))
    return pl.pallas_call(
        functools.partial(_merge_kernel, d=d),
        grid=(rows // tm,),
        in_specs=[rowspec(d), rowspec(BRANCH_W), rowspec(BRANCH_W), rowspec(BRANCH_W), rowspec(N_BRANCH * d),
                  pl.BlockSpec((None, N_BRANCH, BRANCH_W, d), lambda i: (layer, 0, 0, 0)),
                  pl.BlockSpec((None, d, d), lambda i: (layer, 0, 0))],
        out_specs=rowspec(d),
        out_shape=jax.ShapeDtypeStruct((rows, d), F32),
        compiler_params=_params("parallel"),
        name="merge",
    )(x, ya, yb, yc, z, w_branch, w_out)


def _ffn_kernel(x_ref, g_ref, wu_ref, wd_ref, o_ref, hf_s, acc_s):
    j = pl.program_id(1)

    @pl.when(j == 0)
    def _():
        hf_s[...] = _rms_rows(x_ref[...], g_ref[...]).astype(BF16)
        acc_s[...] = jnp.zeros_like(acc_s)

    up = jnp.dot(hf_s[...], wu_ref[...], preferred_element_type=F32)
    act = jnp.square(jnp.maximum(up, 0.0))
    acc_s[...] += jnp.dot(act.astype(BF16), wd_ref[...], preferred_element_type=F32)

    @pl.when(j == pl.num_programs(1) - 1)
    def _():
        o_ref[...] = x_ref[...] + acc_s[...]


def _ffn(x, g, w_up, w_down, layer, *, tm, tf):
    rows, d = x.shape
    dff = w_up.shape[-1]
    return pl.pallas_call(
        _ffn_kernel,
        grid=(rows // tm, dff // tf),
        in_specs=[pl.BlockSpec((tm, d), lambda i, j: (i, 0)),
                  pl.BlockSpec((None, 1, d), lambda i, j: (layer, 0, 0)),
                  pl.BlockSpec((None, d, tf), lambda i, j: (layer, 0, j)),
                  pl.BlockSpec((None, tf, d), lambda i, j: (layer, j, 0))],
        out_specs=pl.BlockSpec((tm, d), lambda i, j: (i, 0)),
        out_shape=jax.ShapeDtypeStruct((rows, d), F32),
        scratch_shapes=[pltpu.VMEM((tm, d), BF16), pltpu.VMEM((tm, d), F32)],
        compiler_params=_params("parallel", "arbitrary"),
        name="ffn",
    )(x, g, w_up, w_down)


def _final_kernel(x_ref, g_ref, o_ref, *, r0, n_out):
    o_ref[...] = _rms_rows(x_ref[r0:r0 + n_out, :], g_ref[...])


def _final_norm(x3, g, *, r0, n_out):
    nb, tp, d = x3.shape
    return pl.pallas_call(
        functools.partial(_final_kernel, r0=r0, n_out=n_out),
        grid=(nb,),
        in_specs=[pl.BlockSpec((None, tp, d), lambda b: (b, 0, 0)), pl.BlockSpec((1, d), lambda b: (0, 0))],
        out_specs=pl.BlockSpec((None, n_out, d), lambda b: (b, 0, 0)),
        out_shape=jax.ShapeDtypeStruct((nb, n_out, d), F32),
        compiler_params=_params("parallel"),
        name="final_norm",
    )(x3, g)


def _pick_tile(n, pref):
    t = pref
    while n % t:
        t //= 2
    return t


def _layer(x, p, layer, st, *, n_batch, tp, t_valid, prompt):
    d = x.shape[1]
    rows = n_batch * tp
    off = p["off"]
    tm = _pick_tile(rows, 512)
    z = _inproj(x, p["norm_mix"], p["w_in"], layer, tm=tm, n_col_tiles=p["n_col_tiles"])
    k, v, ki, kb, vt, kic = _kvprep(z, off, p["k_norm"], p["idx_k_norm"], layer, n_batch=n_batch, tp=tp)
    k3, v3, ki3 = (a.reshape(n_batch, tp, a.shape[-1]) for a in (k, v, ki))
    k_new, v_new, ki_new = k3[:, :t_valid], v3[:, :t_valid], ki3[:, :t_valid]

    if prompt:
        top_k = min(TOP_K_MAX, (t_valid - N_META) // 4)
        tq, n_q = ROW_PAD, tp // ROW_PAD
        parts, i0 = [], 0
        while i0 < n_q:
            i1 = n_q if n_q - i0 <= DSA_TILES_PER_GROUP + 1 else i0 + DSA_TILES_PER_GROUP
            lp = min(tp, -(-(i1 * tq + CHUNK) // LANES) * LANES)
            assert lp >= top_k
            parts.append(_dsa_t(z, off, kb, vt, kic, p["q_norm"], layer, n_batch=n_batch, tp=tp, tq=tq, q0=i0,
                                n_q=i1 - i0, lp=lp, l_valid=t_valid, top_k=top_k))
            i0 = i1
        ya = jnp.concatenate(parts, axis=1).reshape(rows, ATT_W)
    else:
        past = st[0].shape[1]
        l_valid = past + t_valid
        lp = -(-l_valid // LANES) * LANES
        cat = lambda old, new: jnp.concatenate(
            [old.reshape(n_batch, past, -1).astype(F32), new,
             jnp.zeros((n_batch, lp - l_valid, new.shape[-1]), F32)], axis=1)
        kmat, vmat, kimat = cat(st[0], k_new), cat(st[1], v_new), cat(st[2], ki_new)
        top_k = min(TOP_K_MAX, l_valid // 4)
        tq = t_valid
        ya = _dsa(z, off, kmat, vmat, kimat, p["q_norm"], layer, n_batch=n_batch, rows_per_batch=tp, tq=tq,
                  n_q=1, l_valid=l_valid, top_k=top_k, prompt=False)
        ya = jnp.pad(ya.reshape(n_batch, tq, ATT_W), ((0, 0), (0, tp - tq), (0, 0))).reshape(rows, ATT_W)

    if prompt:
        c0 = jnp.zeros((n_batch, M_HEADS, M_V, M_QK), F32)
        n0 = jnp.zeros((n_batch, M_HEADS, 1, M_QK), F32)
        m0 = jnp.zeros((n_batch, 1, LANES), F32)
        h0 = jnp.zeros((n_batch, 1, LRU_W), F32)
        buf0 = jnp.zeros((n_batch, 8, LRU_W), F32)
    else:
        c0 = st[3].astype(F32)
        n0 = st[4].astype(F32).reshape(n_batch, M_HEADS, 1, M_QK)
        m0 = jnp.pad(st[5].astype(F32), ((0, 0), (0, LANES - M_HEADS))).reshape(n_batch, 1, LANES)
        h0 = st[6].astype(F32).reshape(n_batch, 1, LRU_W)
        buf0 = jnp.pad(st[7].astype(F32), ((0, 0), (8 - (CONV_W - 1), 0), (0, 0)))
    yb, c1, n1, m1 = _mlstm(z, off, p["m_bias_i"], p["m_bias_f"], p["m_norm"], c0, n0, m0, layer,
                            n_batch=n_batch, tp=tp, t_valid=t_valid, lc=ROW_PAD, nbb=_pick_tile(n_batch, MLSTM_SEQS))
    yc, h1, buf1 = _lru(z, off, p["conv_w"], p["conv_b"], p["lru_wa"], p["lru_ba"], p["lru_wx"], p["lru_bx"],
                        p["lru_lambda"], h0, buf0, layer, n_batch=n_batch, tp=tp, t_valid=t_valid, tt=ROW_PAD)

    x = _merge(x, ya, yb, yc, z, off, p["w_branch"], p["w_out"], layer, tm=tm)
    x = _ffn(x, p["norm_ffn"], p["w_up"], p["w_down"], layer, tm=tm, tf=_pick_tile(p["w_up"].shape[-1], 2048))
    state = (k_new.reshape(n_batch, t_valid, N_KV_HEADS, HEAD_DIM),
             v_new.reshape(n_batch, t_valid, N_KV_HEADS, HEAD_DIM), ki_new,
             c1, n1.reshape(n_batch, M_HEADS, M_QK), m1[:, 0, :M_HEADS],
             h1.reshape(n_batch, LRU_W), buf1[:, 8 - (CONV_W - 1):])
    return x, state


def _block_diag(w):
    depth, nb, c, _ = w.shape
    eye = jnp.eye(nb, dtype=w.dtype)
    return jnp.einsum("lncd,nm->lncmd", w, eye).reshape(depth, nb * c, nb * c)


def kernel(x_prompt, x_sample, cache_k, cache_v, cache_idx_k, state_mlstm_c, state_mlstm_n, state_mlstm_m, state_lru_h, state_conv, meta_tokens, norm_mix, w_in, q_norm, k_norm, idx_k_norm, m_bias_i, m_bias_f, m_norm, conv_w, conv_b, lru_wa, lru_ba, lru_wx, lru_bx, lru_lambda, w_branch, w_out, norm_ffn, w_up, w_down, final_norm):
    depth = w_in.shape[0]
    nb, seq, d = x_prompt.shape
    nbs, dseq, _ = x_sample.shape
    n_col_tiles = 3
    w_packed, off, _ = _pack_w_in(w_in, d, n_col_tiles)
    vec3 = lambda a: a.astype(F32).reshape(depth, 1, -1)
    p = dict(off=off, n_col_tiles=n_col_tiles, w_in=w_packed,
             norm_mix=vec3(norm_mix), q_norm=vec3(q_norm), k_norm=vec3(k_norm), idx_k_norm=vec3(idx_k_norm),
             m_bias_i=m_bias_i.astype(F32), m_bias_f=m_bias_f.astype(F32), m_norm=vec3(m_norm),
             conv_w=conv_w.astype(F32), conv_b=vec3(conv_b),
             lru_wa=_block_diag(lru_wa.astype(F32)), lru_ba=vec3(lru_ba),
             lru_wx=_block_diag(lru_wx.astype(F32)), lru_bx=vec3(lru_bx), lru_lambda=vec3(lru_lambda),
             w_branch=w_branch.astype(BF16), w_out=w_out.astype(BF16), norm_ffn=vec3(norm_ffn),
             w_up=w_up.astype(BF16), w_down=w_down.astype(BF16))

    t_p = N_META + seq
    tp_p = -(-t_p // ROW_PAD) * ROW_PAD
    meta = jnp.broadcast_to(meta_tokens[None].astype(x_prompt.dtype), (nb, N_META, d))
    hp = jnp.concatenate([meta, x_prompt, jnp.zeros((nb, tp_p - t_p, d), x_prompt.dtype)], axis=1)
    hp = hp.reshape(nb * tp_p, d)
    tp_s = -(-dseq // ROW_PAD) * ROW_PAD
    hs = jnp.pad(x_sample, ((0, 0), (0, tp_s - dseq), (0, 0))).reshape(nbs * tp_s, d)

    p_states, s_states = [], []
    for l in range(depth):
        hp, stp = _layer(hp, p, l, None, n_batch=nb, tp=tp_p, t_valid=t_p, prompt=True)
        hs, sts = _layer(hs, p, l, (cache_k[l], cache_v[l], cache_idx_k[l], state_mlstm_c[l], state_mlstm_n[l],
                                    state_mlstm_m[l], state_lru_h[l], state_conv[l]),
                         n_batch=nbs, tp=tp_s, t_valid=dseq, prompt=False)
        p_states.append(stp)
        s_states.append(sts)
    fg = final_norm.astype(F32).reshape(1, d)
    y_prompt = _final_norm(hp.reshape(nb, tp_p, d), fg, r0=N_META, n_out=seq)
    y_sample = _final_norm(hs.reshape(nbs, tp_s, d), fg, r0=0, n_out=dseq)
    pouts = [jnp.stack(t) for t in zip(*p_states)]
    souts = [jnp.stack(t) for t in zip(*s_states)]
    return (y_prompt, y_sample, *pouts, *souts)
```

```python
import functools

import numpy as np
import jax
import jax.numpy as jnp
from jax import lax
from jax.experimental import pallas as pl
from jax.experimental.pallas import tpu as pltpu

F32 = jnp.float32
BF16 = jnp.bfloat16
I32 = jnp.int32

CHUNK = 64
N_META = 16
N_HEADS = 8
N_KV_HEADS = 2
HEAD_DIM = 64
N_IDX_HEADS = 4
IDX_DIM = 64
TOP_K_MAX = 256
M_HEADS = 4
M_QK = 64
M_V = 128
LRU_W = 512
LRU_BLOCKS = 8
CONV_W = 4
LRU_C = 8.0
N_BRANCH = 3
BRANCH_W = 512
EPS = 1e-6
NEG = -1e30

ATT_W = N_HEADS * HEAD_DIM
KV_W = N_KV_HEADS * HEAD_DIM
GROUP = N_HEADS // N_KV_HEADS

LANES = 128
ROW_PAD = 128
VMEM_LIMIT = 56 * 1024 * 1024
DSA_TILES_PER_GROUP = 2
MLSTM_SEQS = 1

SM_IK = 0
SM_IW = IDX_DIM
SM_MI = SM_IW + N_IDX_HEADS
SM_MF = SM_MI + M_HEADS
SM_W = 128


def _z_layout(d_model):
    widths = [("gt", N_BRANCH * d_model), ("aq", ATT_W), ("mv", M_HEADS * M_V), ("mo", M_HEADS * M_V),
              ("lx", LRU_W), ("lg", LRU_W), ("iq", N_IDX_HEADS * IDX_DIM), ("mq", M_HEADS * M_QK),
              ("mk", M_HEADS * M_QK), ("ak", KV_W), ("av", KV_W), ("sm", SM_W)]
    off, acc = {}, 0
    for name, w in widths:
        assert acc % w == 0, (name, acc, w)
        off[name] = (acc, w)
        acc += w
    return off, acc


def _pack_w_in(w_in, d_model, n_col_tiles):
    sizes = (ATT_W, KV_W, KV_W, N_IDX_HEADS * IDX_DIM, IDX_DIM, N_IDX_HEADS,
             M_HEADS * M_QK, M_HEADS * M_QK, M_HEADS * M_V, M_HEADS, M_HEADS, M_HEADS * M_V,
             LRU_W, LRU_W, N_BRANCH * d_model)
    names = ("aq", "ak", "av", "iq", "ik", "iw", "mq", "mk", "mv", "mi", "mf", "mo", "lx", "lg", "gt")
    assert w_in.shape[-1] == sum(sizes)
    parts, acc = {}, 0
    for n, s in zip(names, sizes):
        parts[n] = w_in[..., acc:acc + s]
        acc += s
    lead = w_in.shape[:-1]
    small = jnp.concatenate([parts["ik"], parts["iw"], parts["mi"], parts["mf"],
                             jnp.zeros(lead + (SM_W - SM_MF - M_HEADS,), w_in.dtype)], axis=-1)
    parts["sm"] = small
    off, zw = _z_layout(d_model)
    order = sorted(off, key=lambda n: off[n][0])
    cols = [parts[n] for n in order]
    zw_pad = -(-zw // (n_col_tiles * LANES)) * (n_col_tiles * LANES)
    if zw_pad > zw:
        cols.append(jnp.zeros(lead + (zw_pad - zw,), w_in.dtype))
    return jnp.concatenate(cols, axis=-1).astype(BF16), off, zw_pad


def _params(*sem):
    return pltpu.CompilerParams(dimension_semantics=sem, vmem_limit_bytes=VMEM_LIMIT)


def _bdot(a, b):
    return jnp.dot(a.astype(BF16), b.astype(BF16), preferred_element_type=F32)


def _bdot_nt(a, b):
    return lax.dot_general(a.astype(BF16), b.astype(BF16), (((1,), (1,)), ((), ())),
                           preferred_element_type=F32)


def _split3(a):
    a1 = a.astype(BF16)
    r1 = a - a1.astype(F32)
    a2 = r1.astype(BF16)
    a3 = (r1 - a2.astype(F32)).astype(BF16)
    return a1, a2, a3


def _dot3_nt(a, b):
    a1, a2, _ = _split3(a)
    b1, b2, _ = _split3(b)
    dn = (((1,), (1,)), ((), ()))
    f = lambda x, y: lax.dot_general(x, y, dn, preferred_element_type=F32)
    return f(a1, b1) + (f(a1, b2) + f(a2, b1))


def _dot3(a, b):
    a1, a2, _ = _split3(a)
    b1, b2, _ = _split3(b)
    f = lambda x, y: jnp.dot(x, y, preferred_element_type=F32)
    return f(a1, b1) + (f(a1, b2) + f(a2, b1))


def _rms_rows(x, g):
    return x * lax.rsqrt(jnp.mean(x * x, axis=-1, keepdims=True) + EPS) * g


def _sigmoid(x):
    return 1.0 / (1.0 + jnp.exp(-x))


def _log_sigmoid(x):
    return jnp.minimum(x, 0.0) - jnp.log1p(jnp.exp(-jnp.abs(x)))


def _softplus(x):
    return jnp.maximum(x, 0.0) + jnp.log1p(jnp.exp(-jnp.abs(x)))


def _gelu_tanh(x):
    c = np.float32(np.sqrt(2.0 / np.pi))
    return 0.5 * x * (1.0 + jnp.tanh(c * (x + np.float32(0.044715) * (x * x * x))))


def _inproj_kernel(x_ref, g_ref, w_ref, z_ref):
    hn = _rms_rows(x_ref[...], g_ref[...])
    z_ref[...] = jnp.dot(hn.astype(BF16), w_ref[...], preferred_element_type=F32)


def _inproj(x, g, w_packed, layer, *, tm, n_col_tiles):
    rows, d = x.shape
    zw = w_packed.shape[-1]
    tn = zw // n_col_tiles
    return pl.pallas_call(
        _inproj_kernel,
        grid=(n_col_tiles, rows // tm),
        in_specs=[pl.BlockSpec((tm, d), lambda j, i: (i, 0)),
                  pl.BlockSpec((None, 1, d), lambda j, i: (layer, 0, 0)),
                  pl.BlockSpec((None, d, tn), lambda j, i: (layer, 0, j))],
        out_specs=pl.BlockSpec((tm, tn), lambda j, i: (i, j)),
        out_shape=jax.ShapeDtypeStruct((rows, zw), F32),
        compiler_params=_params("parallel", "parallel"),
        name="inproj",
    )(x, g, w_packed)


def _kvprep_kernel(ak_ref, av_ref, sm_ref, kn_ref, ikn_ref, k_ref, v_ref, ki_ref, kb_ref, vt_ref, kic_ref):
    ak = ak_ref[...]
    kn = kn_ref[...]
    ks = [_rms_rows(ak[:, h * HEAD_DIM:(h + 1) * HEAD_DIM], kn) for h in range(N_KV_HEADS)]
    k_ref[...] = jnp.concatenate(ks, axis=-1)
    av = av_ref[...]
    v_ref[...] = av
    ki = _rms_rows(sm_ref[...][:, SM_IK:SM_IK + IDX_DIM], ikn_ref[...])
    ki_ref[...] = ki
    for g in range(N_KV_HEADS):
        kb_ref[g] = ks[g].astype(BF16)
    vt_ref[...] = av.T.astype(BF16)
    hi = ki.astype(BF16).astype(F32)
    kic_ref[...] = jnp.concatenate([hi, ki - hi, hi, jnp.zeros_like(hi)], axis=-1).astype(BF16)


def _kvprep(z, off, k_norm, idx_k_norm, layer, *, n_batch, tp):
    rows = z.shape[0]
    cb = lambda name: off[name][0] // off[name][1]
    return pl.pallas_call(
        _kvprep_kernel,
        grid=(n_batch,),
        in_specs=[pl.BlockSpec((tp, KV_W), lambda b: (b, cb("ak"))),
                  pl.BlockSpec((tp, KV_W), lambda b: (b, cb("av"))),
                  pl.BlockSpec((tp, SM_W), lambda b: (b, cb("sm"))),
                  pl.BlockSpec((None, 1, HEAD_DIM), lambda b: (layer, 0, 0)),
                  pl.BlockSpec((None, 1, IDX_DIM), lambda b: (layer, 0, 0))],
        out_specs=[pl.BlockSpec((tp, KV_W), lambda b: (b, 0)),
                   pl.BlockSpec((tp, KV_W), lambda b: (b, 0)),
                   pl.BlockSpec((tp, IDX_DIM), lambda b: (b, 0)),
                   pl.BlockSpec((None, N_KV_HEADS, tp, HEAD_DIM), lambda b: (b, 0, 0, 0)),
                   pl.BlockSpec((None, KV_W, tp), lambda b: (b, 0, 0)),
                   pl.BlockSpec((None, tp, 4 * IDX_DIM), lambda b: (b, 0, 0))],
        out_shape=[jax.ShapeDtypeStruct((rows, KV_W), F32),
                   jax.ShapeDtypeStruct((rows, KV_W), F32),
                   jax.ShapeDtypeStruct((rows, IDX_DIM), F32),
                   jax.ShapeDtypeStruct((n_batch, N_KV_HEADS, tp, HEAD_DIM), BF16),
                   jax.ShapeDtypeStruct((n_batch, KV_W, tp), BF16),
                   jax.ShapeDtypeStruct((n_batch, tp, 4 * IDX_DIM), BF16)],
        compiler_params=_params("parallel"),
        name="kvprep",
    )(z, z, z, k_norm, idx_k_norm)


_INT_MIN = np.int32(-2 ** 31)


def _sort_key_of(x):
    b = np.array(x, np.float32).view(np.int32)
    return np.int32(b ^ 0x7FFFFFFF) if b < 0 else np.int32(b)


_OK_KEY = _sort_key_of(0.5 * NEG)


def _dsa_kernel(aq_ref, iq_ref, sm_ref, k_ref, v_ref, ki_ref, qn_ref, o_ref, key_ref, msk_ref,
                *, tq, lp, l_valid, top_k, prompt):
    nblk = lp // LANES
    kf = np.float32(top_k)

    iq = iq_ref[...]
    sm = sm_ref[...]
    ki = ki_ref[...]
    sc = None
    for h in range(N_IDX_HEADS):
        s = _dot3_nt(iq[:, h * IDX_DIM:(h + 1) * IDX_DIM], ki)
        s = jnp.maximum(s * np.float32(IDX_DIM ** -0.5), 0.0)
        w = sm[:, SM_IW + h:SM_IW + h + 1] * np.float32(N_IDX_HEADS ** -0.5)
        sc = s * w if sc is None else sc + s * w
    sc = jnp.where(sc == 0.0, 0.0, sc)

    kpos = lax.broadcasted_iota(I32, (1, lp), 1)
    if prompt:
        qpos = pl.program_id(1) * tq + lax.broadcasted_iota(I32, (tq, 1), 0)
        allowed = ((kpos + (CHUNK - N_META)) // CHUNK <= (qpos + (CHUNK - N_META)) // CHUNK) & (kpos < l_valid)
    else:
        allowed = jnp.broadcast_to(kpos < l_valid, (tq, lp))
    scm = jnp.where(allowed, sc, np.float32(NEG))
    bits = pltpu.bitcast(scm, I32)
    key_ref[...] = jnp.where(bits < 0, bits ^ np.int32(0x7FFFFFFF), bits)

    def count_ge(cand):
        acc = jnp.zeros((tq, LANES), F32)
        for j in range(nblk):
            acc = acc + jnp.where(key_ref[:, j * LANES:(j + 1) * LANES] >= cand, 1.0, 0.0)
        return jnp.sum(acc, axis=-1, keepdims=True)

    ans = jnp.where(count_ge(jnp.zeros((tq, 1), I32)) >= kf, np.int32(0), _INT_MIN)

    def bit_step(t, ans):
        cand = ans | lax.shift_left(np.int32(1), np.int32(30) - t)
        return jnp.where(count_ge(cand) >= kf, cand, ans)

    thr = lax.fori_loop(0, 31, bit_step, ans)

    acc = jnp.zeros((tq, LANES), F32)
    for j in range(nblk):
        acc = acc + jnp.where(key_ref[:, j * LANES:(j + 1) * LANES] > thr, 1.0, 0.0)
    need = kf - jnp.sum(acc, axis=-1, keepdims=True)
    tri = (lax.broadcasted_iota(I32, (LANES, LANES), 0) < lax.broadcasted_iota(I32, (LANES, LANES), 1))
    tri = jnp.where(tri, 1.0, 0.0).astype(BF16)
    seen = jnp.zeros((tq, 1), F32)
    for j in range(nblk):
        kj = key_ref[:, j * LANES:(j + 1) * LANES]
        eq = jnp.where(kj == thr, 1.0, 0.0)
        before = seen + jnp.dot(eq.astype(BF16), tri, preferred_element_type=F32)
        sel = (kj > thr) | ((kj == thr) & (before < need))
        sel = sel & (kj > _OK_KEY)
        msk_ref[:, j * LANES:(j + 1) * LANES] = jnp.where(sel, 1.0, 0.0)
        seen = seen + jnp.sum(eq, axis=-1, keepdims=True)

    q = aq_ref[...]
    qn = qn_ref[...]
    kk = k_ref[...]
    vv = v_ref[...]
    sel = msk_ref[...] > 0.5
    outs = []
    for g in range(N_KV_HEADS):
        kg = kk[:, g * HEAD_DIM:(g + 1) * HEAD_DIM].astype(BF16)
        vg = vv[:, g * HEAD_DIM:(g + 1) * HEAD_DIM].astype(BF16)
        for hh in range(GROUP):
            h = g * GROUP + hh
            qh = _rms_rows(q[:, h * HEAD_DIM:(h + 1) * HEAD_DIM], qn)
            s = _bdot_nt(qh, kg) * np.float32(HEAD_DIM ** -0.5)
            s = jnp.where(sel, s, np.float32(NEG))
            p = jnp.exp(s - jnp.max(s, axis=-1, keepdims=True))
            den = jnp.sum(p, axis=-1, keepdims=True)
            outs.append(jnp.dot(p.astype(BF16), vg, preferred_element_type=F32) / den)
    o_ref[...] = jnp.concatenate(outs, axis=-1)


def _dsa(z, off, kmat, vmat, kimat, q_norm, layer, *, n_batch, rows_per_batch, tq, n_q, l_valid, top_k, prompt):
    lp = kmat.shape[1]
    assert lp % LANES == 0 and rows_per_batch % tq == 0
    rb = rows_per_batch // tq
    cb = lambda name: off[name][0] // off[name][1]
    kern = functools.partial(_dsa_kernel, tq=tq, lp=lp, l_valid=l_valid, top_k=top_k, prompt=prompt)
    return pl.pallas_call(
        kern,
        grid=(n_batch, n_q),
        in_specs=[pl.BlockSpec((tq, ATT_W), lambda b, i: (b * rb + i, cb("aq"))),
                  pl.BlockSpec((tq, N_IDX_HEADS * IDX_DIM), lambda b, i: (b * rb + i, cb("iq"))),
                  pl.BlockSpec((tq, SM_W), lambda b, i: (b * rb + i, cb("sm"))),
                  pl.BlockSpec((None, lp, KV_W), lambda b, i: (b, 0, 0)),
                  pl.BlockSpec((None, lp, KV_W), lambda b, i: (b, 0, 0)),
                  pl.BlockSpec((None, lp, IDX_DIM), lambda b, i: (b, 0, 0)),
                  pl.BlockSpec((None, 1, HEAD_DIM), lambda b, i: (layer, 0, 0))],
        out_specs=pl.BlockSpec((tq, ATT_W), lambda b, i: (b * n_q + i, 0)),
        out_shape=jax.ShapeDtypeStruct((n_batch * n_q * tq, ATT_W), F32),
        scratch_shapes=[pltpu.VMEM((tq, lp), I32), pltpu.VMEM((tq, lp), F32)],
        compiler_params=_params("parallel", "parallel"),
        name="dsa",
    )(z, z, z, kmat, vmat, kimat, q_norm)


def _row_blocks(x):
    return [x[j * LANES:(j + 1) * LANES, :] for j in range(x.shape[0] // LANES)]


def _col_reduce(blocks, op, red):
    a = blocks[0]
    for blk in blocks[1:]:
        a = op(a, blk)
    r = a.shape[0]
    while r > 8:
        r //= 2
        a = op(a[:r], a[r:])
    return red(a, axis=0, keepdims=True)


_WORD = 32
_SUB = 8


def _bit_planes(slabs):
    x = list(slabs)
    for d, m in ((16, 0x0000FFFF), (8, 0x00FF00FF), (4, 0x0F0F0F0F), (2, 0x33333333), (1, 0x55555555)):
        for i in range(_WORD):
            if not i & d:
                a, b = x[i], x[i + d]
                t = (lax.shift_right_logical(a, np.int32(d)) ^ b) & np.int32(m)
                x[i + d] = b ^ t
                x[i] = a ^ lax.shift_left(t, np.int32(d))
    return x


def _kth_largest(key_ref, lp, tq, top_k):
    n_pieces = lp // _SUB
    n_grp = -(-n_pieces // _WORD)
    slabs = []
    for i in range(_WORD):
        parts = []
        for g in range(n_grp):
            r0 = (g * _WORD + i) * _SUB
            parts.append(key_ref[r0:r0 + _SUB, :] ^ _INT_MIN if r0 < lp else jnp.zeros((_SUB, tq), I32))
        slabs.append(jnp.concatenate(parts, axis=0) if n_grp > 1 else parts[0])
    planes = _bit_planes(slabs)
    grp = lax.broadcasted_iota(I32, (n_grp * _SUB, tq), 0) // _SUB
    last = n_pieces - (n_grp - 1) * _WORD
    last_mask = np.int32(-1) if last == _WORD else np.int32((1 << last) - 1)
    act = jnp.where(grp < n_grp - 1, np.int32(-1), last_mask)
    above = jnp.zeros((1, tq), I32)
    thr_u = jnp.zeros((1, tq), I32)
    for b in range(_WORD - 1, -1, -1):
        ones = act & planes[b]
        n1 = jnp.sum(lax.population_count(ones), axis=0, keepdims=True)
        take = above + n1 >= top_k
        act = jnp.where(take, ones, act ^ ones)
        above = jnp.where(take, above, above + n1)
        thr_u = jnp.where(take, thr_u | (_INT_MIN if b == _WORD - 1 else np.int32(1 << b)), thr_u)
    n_eq = jnp.sum(lax.population_count(act), axis=0, keepdims=True)
    return thr_u ^ _INT_MIN, above + n_eq


def _order_key(bits):
    return jnp.where(bits < 0, _INT_MIN - bits, bits)


_OK_KEY_T = np.int32(-int(np.array(-0.5 * NEG, np.float32).view(np.int32)))


def _dsa_t_kernel(aq_ref, iq_ref, sm_ref, kb_ref, vt_ref, kic_ref, qn_ref, o_ref, key_ref, bias_ref,
                  *, tq, lp, l_valid, top_k, q0):
    kf = np.float32(top_k)
    nt = (((1,), (1,)), ((), ()))

    smt = sm_ref[...].T
    iq = iq_ref[...]
    pieces = []
    for h in range(N_IDX_HEADS):
        qh = iq[:, h * IDX_DIM:(h + 1) * IDX_DIM]
        hi = qh.astype(BF16).astype(F32)
        pieces.append(jnp.concatenate([hi, hi, qh - hi, jnp.zeros_like(hi)], axis=-1))
    qcat = jnp.concatenate(pieces, axis=0).astype(BF16)
    st = lax.dot_general(kic_ref[...], qcat, nt, preferred_element_type=F32)
    sc = None
    for h in range(N_IDX_HEADS):
        w = smt[SM_IW + h:SM_IW + h + 1, :] * np.float32(N_IDX_HEADS ** -0.5 * IDX_DIM ** -0.5)
        t = jnp.maximum(st[:, h * tq:(h + 1) * tq], 0.0) * w
        sc = t if sc is None else sc + t

    kpos = lax.broadcasted_iota(I32, (lp, 1), 0)
    qpos = (q0 + pl.program_id(1)) * tq + lax.broadcasted_iota(I32, (1, tq), 1)
    allowed = ((kpos + (CHUNK - N_META)) // CHUNK <= (qpos + (CHUNK - N_META)) // CHUNK) & (kpos < l_valid)
    scm = jnp.where(allowed, sc, np.float32(NEG))
    key_ref[...] = _order_key(pltpu.bitcast(scm, I32))

    thr, n_ge = _kth_largest(key_ref, lp, tq, top_k)
    exact = (jnp.min(jnp.where(n_ge == top_k, 1.0, 0.0)) > 0.5).astype(I32)

    @pl.when(exact == 1)
    def _():
        key = key_ref[...]
        bias_ref[...] = jnp.where((key >= thr) & (key > _OK_KEY_T), 0.0, np.float32(NEG))

    @pl.when(exact == 0)
    def _():
        need = kf - _col_reduce([jnp.where(key_ref[j * LANES:(j + 1) * LANES, :] > thr, 1.0, 0.0)
                                 for j in range(lp // LANES)], jnp.add, jnp.sum)
        tri = (lax.broadcasted_iota(I32, (LANES, LANES), 1) < lax.broadcasted_iota(I32, (LANES, LANES), 0))
        tri = jnp.where(tri, 1.0, 0.0).astype(BF16)
        seen = jnp.zeros((1, tq), F32)
        for j in range(lp // LANES):
            kj = key_ref[j * LANES:(j + 1) * LANES, :]
            eq = jnp.where(kj == thr, 1.0, 0.0)
            before = seen + jnp.dot(tri, eq.astype(BF16), preferred_element_type=F32)
            sel = ((kj > thr) | ((kj == thr) & (before < need))) & (kj > _OK_KEY_T)
            bias_ref[j * LANES:(j + 1) * LANES, :] = jnp.where(sel, 0.0, np.float32(NEG))
            seen = seen + jnp.sum(eq, axis=0, keepdims=True)

    q = aq_ref[...]
    qn = qn_ref[...] * np.float32(HEAD_DIM ** -0.5 * np.log2(np.e))
    bias = bias_ref[...]
    outs = []
    for g in range(N_KV_HEADS):
        qs = [_rms_rows(q[:, h * HEAD_DIM:(h + 1) * HEAD_DIM], qn) for h in range(g * GROUP, (g + 1) * GROUP)]
        qs = jnp.concatenate(qs, axis=0).astype(BF16)
        s_all = lax.dot_general(kb_ref[g], qs, nt, preferred_element_type=F32)
        vtg = vt_ref[g * HEAD_DIM:(g + 1) * HEAD_DIM, :]
        for hh in range(GROUP):
            s = s_all[:, hh * tq:(hh + 1) * tq] + bias
            p = jnp.exp2(s - _col_reduce(_row_blocks(s), jnp.maximum, jnp.max))
            den = _col_reduce(_row_blocks(p), jnp.add, jnp.sum)
            outs.append(jnp.dot(vtg, p.astype(BF16), preferred_element_type=F32) / den)
    o_ref[...] = jnp.concatenate(outs, axis=0).T


def _dsa_t(z, off, kb, vt, kic, q_norm, layer, *, n_batch, tp, tq, q0, n_q, lp, l_valid, top_k):
    rb = tp // tq
    cb = lambda name: off[name][0] // off[name][1]
    kern = functools.partial(_dsa_t_kernel, tq=tq, lp=lp, l_valid=l_valid, top_k=top_k, q0=q0)
    qrow = lambda b, i: b * rb + q0 + i
    return pl.pallas_call(
        kern,
        grid=(n_batch, n_q),
        in_specs=[pl.BlockSpec((tq, ATT_W), lambda b, i: (qrow(b, i), cb("aq"))),
                  pl.BlockSpec((tq, N_IDX_HEADS * IDX_DIM), lambda b, i: (qrow(b, i), cb("iq"))),
                  pl.BlockSpec((tq, SM_W), lambda b, i: (qrow(b, i), cb("sm"))),
                  pl.BlockSpec((None, N_KV_HEADS, lp, HEAD_DIM), lambda b, i: (b, 0, 0, 0)),
                  pl.BlockSpec((None, KV_W, lp), lambda b, i: (b, 0, 0)),
                  pl.BlockSpec((None, lp, 4 * IDX_DIM), lambda b, i: (b, 0, 0)),
                  pl.BlockSpec((None, 1, HEAD_DIM), lambda b, i: (layer, 0, 0))],
        out_specs=pl.BlockSpec((None, tq, ATT_W), lambda b, i: (b, i, 0)),
        out_shape=jax.ShapeDtypeStruct((n_batch, n_q * tq, ATT_W), F32),
        scratch_shapes=[pltpu.VMEM((lp, tq), I32), pltpu.VMEM((lp, tq), F32)],
        compiler_params=_params("parallel", "parallel"),
        name="dsa_t",
    )(z, z, z, kb, vt, kic, q_norm)


def _mlstm_kernel(bi_ref, bf_ref, mq_ref, mk_ref, mv_ref, mo_ref, sm_ref, gn_ref, c0_ref, n0_ref, m0_ref,
                  yb_ref, c1_ref, n1_ref, m1_ref, c_s, n_s, m_s, *, lc, t_valid, layer, nbb):
    j = pl.program_id(1)

    @pl.when(j == 0)
    def _():
        c_s[...] = c0_ref[...]
        n_s[...] = n0_ref[...]
        m_s[...] = m0_ref[...]

    t_io = lax.broadcasted_iota(I32, (lc, lc), 0)
    s_io = lax.broadcasted_iota(I32, (lc, lc), 1)
    causal = s_io <= t_io
    pos_col = j * lc + lax.broadcasted_iota(I32, (lc, 1), 0)
    pos_row = j * lc + lax.broadcasted_iota(I32, (1, lc), 1)
    lane = lax.broadcasted_iota(I32, (1, LANES), 1)
    gn = gn_ref[...]
    c_in = [[c_s[bb, h] for h in range(M_HEADS)] for bb in range(nbb)]
    n_in = [[n_s[bb, h] for h in range(M_HEADS)] for bb in range(nbb)]
    m_in = [m_s[bb] for bb in range(nbb)]
    c_out, n_out, m_outs, y_out = [], [], [], []
    for bb in range(nbb):
        sm = sm_ref[bb]
        smt = sm.T
        mq = mq_ref[bb]
        mk = mk_ref[bb]
        mv = mv_ref[bb]
        mo = mo_ref[bb]
        m_all = m_in[bb]
        m_out = m_all
        for h in range(M_HEADS):
            bi = bi_ref[layer, h]
            bf = bf_ref[layer, h]
            i_col = jnp.where(pos_col < t_valid, sm[:, SM_MI + h:SM_MI + h + 1] + bi, np.float32(NEG))
            f_col = jnp.where(pos_col < t_valid, _log_sigmoid(sm[:, SM_MF + h:SM_MF + h + 1] + bf), 0.0)
            i_row = jnp.where(pos_row < t_valid, smt[SM_MI + h:SM_MI + h + 1, :] + bi, np.float32(NEG))
            f_row = jnp.where(pos_row < t_valid, _log_sigmoid(smt[SM_MF + h:SM_MF + h + 1, :] + bf), 0.0)
            b_col = jnp.sum(jnp.where(causal, f_row, 0.0), axis=1, keepdims=True)
            b_row = jnp.sum(jnp.where(t_io <= s_io, f_col, 0.0), axis=0, keepdims=True)
            m_prev = m_all[:, h:h + 1]
            log_d = jnp.where(causal, b_col - b_row + i_row, np.float32(NEG))
            m_t = jnp.maximum(b_col + m_prev, jnp.max(log_d, axis=1, keepdims=True))
            d = jnp.exp(log_d - m_t)
            inter = jnp.exp(b_col + m_prev - m_t)
            qh = mq[:, h * M_QK:(h + 1) * M_QK]
            kh = mk[:, h * M_QK:(h + 1) * M_QK] * np.float32(M_QK ** -0.5)
            vh = mv[:, h * M_V:(h + 1) * M_V]
            c_h = c_in[bb][h]
            n_h = n_in[bb][h]
            w = _bdot_nt(qh, kh) * d
            num = _bdot(w, vh) + inter * _bdot_nt(qh, c_h)
            den = jnp.sum(w, axis=1, keepdims=True) + inter * jnp.sum(qh * n_h, axis=1, keepdims=True)
            hm = num / jnp.maximum(jnp.abs(den), jnp.exp(-m_t))
            m_new = m_t[lc - 1:lc, :]
            b_last = b_col[lc - 1:lc, :]
            decay = jnp.exp(b_last + m_prev - m_new)
            ws = jnp.exp(b_last - b_col + i_col - m_new)
            c_out.append(decay * c_h + _bdot((vh * ws).T, kh))
            n_out.append(decay * n_h + jnp.sum(ws * kh, axis=0, keepdims=True))
            m_out = jnp.where(lane == h, m_new, m_out)
            y = _rms_rows(hm, gn[:, h * M_V:(h + 1) * M_V])
            y_out.append(y * _sigmoid(mo[:, h * M_V:(h + 1) * M_V]))
        m_outs.append(m_out)
    for bb in range(nbb):
        for h in range(M_HEADS):
            c_s[bb, h] = c_out[bb * M_HEADS + h]
            n_s[bb, h] = n_out[bb * M_HEADS + h]
            yb_ref[bb, :, h * M_V:(h + 1) * M_V] = y_out[bb * M_HEADS + h]
        m_s[bb] = m_outs[bb]

    @pl.when(j == pl.num_programs(1) - 1)
    def _():
        c1_ref[...] = c_s[...]
        n1_ref[...] = n_s[...]
        m1_ref[...] = m_s[...]


def _mlstm(z, off, bias_i, bias_f, m_norm, c0, n0, m0, layer, *, n_batch, tp, t_valid, lc, nbb):
    assert n_batch % nbb == 0
    nc = tp // lc
    z3 = z.reshape(n_batch, tp, z.shape[-1])
    cb = lambda name: off[name][0] // off[name][1]
    row = lambda name: pl.BlockSpec((nbb, lc, off[name][1]), lambda b, j: (b, j, cb(name)))
    smem = pl.BlockSpec(memory_space=pltpu.SMEM)
    st_c = pl.BlockSpec((nbb, M_HEADS, M_V, M_QK), lambda b, j: (b, 0, 0, 0))
    st_n = pl.BlockSpec((nbb, M_HEADS, 1, M_QK), lambda b, j: (b, 0, 0, 0))
    st_m = pl.BlockSpec((nbb, 1, LANES), lambda b, j: (b, 0, 0))
    kern = functools.partial(_mlstm_kernel, lc=lc, t_valid=t_valid, layer=layer, nbb=nbb)
    yb, c1, n1, m1 = pl.pallas_call(
        kern,
        grid=(n_batch // nbb, nc),
        in_specs=[smem, smem, row("mq"), row("mk"), row("mv"), row("mo"), row("sm"),
                  pl.BlockSpec((None, 1, M_HEADS * M_V), lambda b, j: (layer, 0, 0)),
                  st_c, st_n, st_m],
        out_specs=[pl.BlockSpec((nbb, lc, M_HEADS * M_V), lambda b, j: (b, j, 0)), st_c, st_n, st_m],
        out_shape=[jax.ShapeDtypeStruct((n_batch, tp, M_HEADS * M_V), F32),
                   jax.ShapeDtypeStruct(c0.shape, F32), jax.ShapeDtypeStruct(n0.shape, F32),
                   jax.ShapeDtypeStruct(m0.shape, F32)],
        scratch_shapes=[pltpu.VMEM((nbb, M_HEADS, M_V, M_QK), F32), pltpu.VMEM((nbb, M_HEADS, 1, M_QK), F32),
                        pltpu.VMEM((nbb, 1, LANES), F32)],
        compiler_params=_params("parallel", "arbitrary"),
        name="mlstm",
    )(bias_i, bias_f, z3, z3, z3, z3, z3, m_norm, c0, n0, m0)
    return yb.reshape(n_batch * tp, M_HEADS * M_V), c1, n1, m1


def _shift_rows_in_group(x, s, fill):
    rolled = pltpu.roll(x, s, 0)
    row = lax.broadcasted_iota(I32, x.shape, 0)
    return jnp.where((row & 7) >= s, rolled, fill)


def _lru_kernel(lx_ref, lg_ref, cw_ref, cb_ref, w3_ref, ba_ref, bx_ref, lam_ref, h0_ref, buf0_ref,
                yc_ref, h1_ref, buf1_ref, prev_s, h_s, *, tt, t_valid):
    j = pl.program_id(1)

    @pl.when(j == 0)
    def _():
        prev_s[...] = buf0_ref[...]
        h_s[...] = h0_ref[...]

    lx = lx_ref[...]
    cw = cw_ref[...]
    prev = prev_s[...]
    row8 = lax.broadcasted_iota(I32, prev.shape, 0)
    xc = cb_ref[...] + lx * cw[CONV_W - 1:CONV_W, :]
    for s in range(1, CONV_W):
        rolled = pltpu.roll(lx, s, 0)
        head = jnp.where(row8 < s, pltpu.roll(prev, s, 0), rolled[:8])
        shifted = jnp.concatenate([head, rolled[8:]], axis=0) if tt > 8 else head
        xc = xc + shifted * cw[CONV_W - 1 - s:CONV_W - s, :]
    prev_s[...] = lx[tt - 8:, :]

    x_hi = xc.astype(BF16)
    x_lo = (xc - x_hi.astype(F32)).astype(BF16)
    gates = jnp.dot(jnp.concatenate([x_hi, x_hi, x_lo], axis=1), w3_ref[...], preferred_element_type=F32)
    r = _sigmoid(gates[:, :LRU_W] + ba_ref[...])
    gi = _sigmoid(gates[:, LRU_W:] + bx_ref[...])
    log_a = np.float32(-LRU_C) * r * _softplus(-lam_ref[...])
    a = jnp.exp(log_a)
    u = jnp.sqrt(jnp.tanh(-log_a) * (a * a + 1.0)) * gi * xc
    s = 1
    while s < 8:
        u = a * _shift_rows_in_group(u, s, 0.0) + u
        a = a * _shift_rows_in_group(a, s, 1.0)
        s *= 2
    carry = h_s[...]
    hs = []
    for g in range(tt // 8):
        hg = a[g * 8:(g + 1) * 8, :] * carry + u[g * 8:(g + 1) * 8, :]
        carry = hg[7:8, :]
        hs.append(hg)
    h = jnp.concatenate(hs, axis=0)
    h_s[...] = carry
    yc_ref[...] = h * _gelu_tanh(lg_ref[...])

    last = t_valid - 1

    @pl.when(j == last // tt)
    def _():
        r0 = last % tt
        h1_ref[...] = h[r0:r0 + 1, :]
        buf1_ref[...] = lx[r0 - 7:r0 + 1, :]


def _lru(z, off, conv_w, conv_b, w3, ba, bx, lam, h0, buf0, layer, *, n_batch, tp, t_valid, tt):
    assert t_valid % 8 == 0 and tt % 8 == 0
    nt = tp // tt
    cb = lambda name: off[name][0] // off[name][1]
    row = lambda name: pl.BlockSpec((tt, LRU_W), lambda b, j: (b * nt + j, cb(name)))
    vec = lambda n: pl.BlockSpec((None, n, LRU_W), lambda b, j: (layer, 0, 0))
    st_h = pl.BlockSpec((None, 1, LRU_W), lambda b, j: (b, 0, 0))
    st_b = pl.BlockSpec((None, 8, LRU_W), lambda b, j: (b, 0, 0))
    kern = functools.partial(_lru_kernel, tt=tt, t_valid=t_valid)
    return pl.pallas_call(
        kern,
        grid=(n_batch, nt),
        in_specs=[row("lx"), row("lg"), vec(CONV_W), vec(1),
                  pl.BlockSpec((None, 3 * LRU_W, 2 * LRU_W), lambda b, j: (layer, 0, 0)), vec(1), vec(1), vec(1),
                  st_h, st_b],
        out_specs=[pl.BlockSpec((tt, LRU_W), lambda b, j: (b * nt + j, 0)), st_h, st_b],
        out_shape=[jax.ShapeDtypeStruct((n_batch * tp, LRU_W), F32),
                   jax.ShapeDtypeStruct(h0.shape, F32), jax.ShapeDtypeStruct(buf0.shape, F32)],
        scratch_shapes=[pltpu.VMEM((8, LRU_W), F32), pltpu.VMEM((1, LRU_W), F32)],
        compiler_params=_params("parallel", "arbitrary"),
        name="lru",
    )(z, z, conv_w, conv_b, w3, ba, bx, lam, h0, buf0)


def _merge_kernel(x_ref, ya_ref, yb_ref, yc_ref, gt_ref, wb_ref, wo_ref, o_ref, *, d):
    acc = None
    for b, y_ref in enumerate((ya_ref, yb_ref, yc_ref)):
        proj = jnp.dot(y_ref[...].astype(BF16), wb_ref[b], preferred_element_type=F32)
        t = _sigmoid(gt_ref[:, b * d:(b + 1) * d]) * proj
        acc = t if acc is None else acc + t
    o_ref[...] = x_ref[...] + jnp.dot(acc.astype(BF16), wo_ref[...], preferred_element_type=F32)


def _merge(x, ya, yb, yc, z, off, w_branch, w_out, layer, *, tm):
    rows, d = x.shape
    assert off["gt"][0] == 0
    rowspec = lambda w: pl.BlockSpec((tm, w), lambda i: (i, 0))
    return pl.pallas_call(
        functools.partial(_merge_kernel, d=d),
        grid=(rows // tm,),
        in_specs=[rowspec(d), rowspec(BRANCH_W), rowspec(BRANCH_W), rowspec(BRANCH_W), rowspec(N_BRANCH * d),
                  pl.BlockSpec((None, N_BRANCH, BRANCH_W, d), lambda i: (layer, 0, 0, 0)),
                  pl.BlockSpec((None, d, d), lambda i: (layer, 0, 0))],
        out_specs=rowspec(d),
        out_shape=jax.ShapeDtypeStruct((rows, d), F32),
        compiler_params=_params("parallel"),
        name="merge",
    )(x, ya, yb, yc, z, w_branch, w_out)


def _ffn_kernel(x_ref, g_ref, wu_ref, wd_ref, o_ref, hf_s, acc_s):
    j = pl.program_id(1)

    @pl.when(j == 0)
    def _():
        hf_s[...] = _rms_rows(x_ref[...], g_ref[...]).astype(BF16)
        acc_s[...] = jnp.zeros_like(acc_s)

    up = jnp.dot(hf_s[...], wu_ref[...], preferred_element_type=F32)
    act = jnp.square(jnp.maximum(up, 0.0))
    acc_s[...] += jnp.dot(act.astype(BF16), wd_ref[...], preferred_element_type=F32)

    @pl.when(j == pl.num_programs(1) - 1)
    def _():
        o_ref[...] = x_ref[...] + acc_s[...]


def _ffn(x, g, w_up, w_down, layer, *, tm, tf):
    rows, d = x.shape
    dff = w_up.shape[-1]
    return pl.pallas_call(
        _ffn_kernel,
        grid=(rows // tm, dff // tf),
        in_specs=[pl.BlockSpec((tm, d), lambda i, j: (i, 0)),
                  pl.BlockSpec((None, 1, d), lambda i, j: (layer, 0, 0)),
                  pl.BlockSpec((None, d, tf), lambda i, j: (layer, 0, j)),
                  pl.BlockSpec((None, tf, d), lambda i, j: (layer, j, 0))],
        out_specs=pl.BlockSpec((tm, d), lambda i, j: (i, 0)),
        out_shape=jax.ShapeDtypeStruct((rows, d), F32),
        scratch_shapes=[pltpu.VMEM((tm, d), BF16), pltpu.VMEM((tm, d), F32)],
        compiler_params=_params("parallel", "arbitrary"),
        name="ffn",
    )(x, g, w_up, w_down)


def _final_kernel(x_ref, g_ref, o_ref, *, r0, n_out):
    o_ref[...] = _rms_rows(x_ref[r0:r0 + n_out, :], g_ref[...])


def _final_norm(x3, g, *, r0, n_out):
    nb, tp, d = x3.shape
    return pl.pallas_call(
        functools.partial(_final_kernel, r0=r0, n_out=n_out),
        grid=(nb,),
        in_specs=[pl.BlockSpec((None, tp, d), lambda b: (b, 0, 0)), pl.BlockSpec((1, d), lambda b: (0, 0))],
        out_specs=pl.BlockSpec((None, n_out, d), lambda b: (b, 0, 0)),
        out_shape=jax.ShapeDtypeStruct((nb, n_out, d), F32),
        compiler_params=_params("parallel"),
        name="final_norm",
    )(x3, g)


def _pick_tile(n, pref):
    t = pref
    while n % t:
        t //= 2
    return t


def _layer(x, p, layer, st, *, n_batch, tp, t_valid, prompt):
    d = x.shape[1]
    rows = n_batch * tp
    off = p["off"]
    tm = _pick_tile(rows, 512)
    z = _inproj(x, p["norm_mix"], p["w_in"], layer, tm=tm, n_col_tiles=p["n_col_tiles"])
    k, v, ki, kb, vt, kic = _kvprep(z, off, p["k_norm"], p["idx_k_norm"], layer, n_batch=n_batch, tp=tp)
    k3, v3, ki3 = (a.reshape(n_batch, tp, a.shape[-1]) for a in (k, v, ki))
    k_new, v_new, ki_new = k3[:, :t_valid], v3[:, :t_valid], ki3[:, :t_valid]

    if prompt:
        top_k = min(TOP_K_MAX, (t_valid - N_META) // 4)
        tq, n_q = ROW_PAD, tp // ROW_PAD
        parts, i0 = [], 0
        while i0 < n_q:
            i1 = n_q if n_q - i0 <= DSA_TILES_PER_GROUP + 1 else i0 + DSA_TILES_PER_GROUP
            lp = min(tp, -(-(i1 * tq + CHUNK) // LANES) * LANES)
            assert lp >= top_k
            parts.append(_dsa_t(z, off, kb, vt, kic, p["q_norm"], layer, n_batch=n_batch, tp=tp, tq=tq, q0=i0,
                                n_q=i1 - i0, lp=lp, l_valid=t_valid, top_k=top_k))
            i0 = i1
        ya = jnp.concatenate(parts, axis=1).reshape(rows, ATT_W)
    else:
        past = st[0].shape[1]
        l_valid = past + t_valid
        lp = -(-l_valid // LANES) * LANES
        cat = lambda old, new: jnp.concatenate(
            [old.reshape(n_batch, past, -1).astype(F32), new,
             jnp.zeros((n_batch, lp - l_valid, new.shape[-1]), F32)], axis=1)
        kmat, vmat, kimat = cat(st[0], k_new), cat(st[1], v_new), cat(st[2], ki_new)
        top_k = min(TOP_K_MAX, l_valid // 4)
        tq = t_valid
        ya = _dsa(z, off, kmat, vmat, kimat, p["q_norm"], layer, n_batch=n_batch, rows_per_batch=tp, tq=tq,
                  n_q=1, l_valid=l_valid, top_k=top_k, prompt=False)
        ya = jnp.pad(ya.reshape(n_batch, tq, ATT_W), ((0, 0), (0, tp - tq), (0, 0))).reshape(rows, ATT_W)

    if prompt:
        c0 = jnp.zeros((n_batch, M_HEADS, M_V, M_QK), F32)
        n0 = jnp.zeros((n_batch, M_HEADS, 1, M_QK), F32)
        m0 = jnp.zeros((n_batch, 1, LANES), F32)
        h0 = jnp.zeros((n_batch, 1, LRU_W), F32)
        buf0 = jnp.zeros((n_batch, 8, LRU_W), F32)
    else:
        c0 = st[3].astype(F32)
        n0 = st[4].astype(F32).reshape(n_batch, M_HEADS, 1, M_QK)
        m0 = jnp.pad(st[5].astype(F32), ((0, 0), (0, LANES - M_HEADS))).reshape(n_batch, 1, LANES)
        h0 = st[6].astype(F32).reshape(n_batch, 1, LRU_W)
        buf0 = jnp.pad(st[7].astype(F32), ((0, 0), (8 - (CONV_W - 1), 0), (0, 0)))
    yb, c1, n1, m1 = _mlstm(z, off, p["m_bias_i"], p["m_bias_f"], p["m_norm"], c0, n0, m0, layer,
                            n_batch=n_batch, tp=tp, t_valid=t_valid, lc=ROW_PAD, nbb=_pick_tile(n_batch, MLSTM_SEQS))
    yc, h1, buf1 = _lru(z, off, p["conv_w"], p["conv_b"], p["lru_w3"], p["lru_ba"], p["lru_bx"],
                        p["lru_lambda"], h0, buf0, layer, n_batch=n_batch, tp=tp, t_valid=t_valid, tt=ROW_PAD)

    x = _merge(x, ya, yb, yc, z, off, p["w_branch"], p["w_out"], layer, tm=tm)
    x = _ffn(x, p["norm_ffn"], p["w_up"], p["w_down"], layer, tm=tm, tf=_pick_tile(p["w_up"].shape[-1], 2048))
    state = (k_new.reshape(n_batch, t_valid, N_KV_HEADS, HEAD_DIM),
             v_new.reshape(n_batch, t_valid, N_KV_HEADS, HEAD_DIM), ki_new,
             c1, n1.reshape(n_batch, M_HEADS, M_QK), m1[:, 0, :M_HEADS],
             h1.reshape(n_batch, LRU_W), buf1[:, 8 - (CONV_W - 1):])
    return x, state


def _block_diag(w):
    depth, nb, c, _ = w.shape
    eye = jnp.eye(nb, dtype=w.dtype)
    return jnp.einsum("lncd,nm->lncmd", w, eye).reshape(depth, nb * c, nb * c)


def _gate_weights(wa, wx):
    w = jnp.concatenate([_block_diag(wa.astype(F32)), _block_diag(wx.astype(F32))], axis=-1)
    hi = w.astype(BF16)
    lo = (w - hi.astype(F32)).astype(BF16)
    return jnp.concatenate([hi, lo, hi], axis=1)


def kernel(x_prompt, x_sample, cache_k, cache_v, cache_idx_k, state_mlstm_c, state_mlstm_n, state_mlstm_m, state_lru_h, state_conv, meta_tokens, norm_mix, w_in, q_norm, k_norm, idx_k_norm, m_bias_i, m_bias_f, m_norm, conv_w, conv_b, lru_wa, lru_ba, lru_wx, lru_bx, lru_lambda, w_branch, w_out, norm_ffn, w_up, w_down, final_norm):
    depth = w_in.shape[0]
    nb, seq, d = x_prompt.shape
    nbs, dseq, _ = x_sample.shape
    n_col_tiles = 3
    w_packed, off, _ = _pack_w_in(w_in, d, n_col_tiles)
    vec3 = lambda a: a.astype(F32).reshape(depth, 1, -1)
    p = dict(off=off, n_col_tiles=n_col_tiles, w_in=w_packed,
             norm_mix=vec3(norm_mix), q_norm=vec3(q_norm), k_norm=vec3(k_norm), idx_k_norm=vec3(idx_k_norm),
             m_bias_i=m_bias_i.astype(F32), m_bias_f=m_bias_f.astype(F32), m_norm=vec3(m_norm),
             conv_w=conv_w.astype(F32), conv_b=vec3(conv_b),
             lru_w3=_gate_weights(lru_wa, lru_wx), lru_ba=vec3(lru_ba), lru_bx=vec3(lru_bx),
             lru_lambda=vec3(lru_lambda),
             w_branch=w_branch.astype(BF16), w_out=w_out.astype(BF16), norm_ffn=vec3(norm_ffn),
             w_up=w_up.astype(BF16), w_down=w_down.astype(BF16))

    t_p = N_META + seq
    tp_p = -(-t_p // ROW_PAD) * ROW_PAD
    meta = jnp.broadcast_to(meta_tokens[None].astype(x_prompt.dtype), (nb, N_META, d))
    hp = jnp.concatenate([meta, x_prompt, jnp.zeros((nb, tp_p - t_p, d), x_prompt.dtype)], axis=1)
    hp = hp.reshape(nb * tp_p, d)
    tp_s = -(-dseq // ROW_PAD) * ROW_PAD
    hs = jnp.pad(x_sample, ((0, 0), (0, tp_s - dseq), (0, 0))).reshape(nbs * tp_s, d)

    p_states, s_states = [], []
    for l in range(depth):
        hp, stp = _layer(hp, p, l, None, n_batch=nb, tp=tp_p, t_valid=t_p, prompt=True)
        hs, sts = _layer(hs, p, l, (cache_k[l], cache_v[l], cache_idx_k[l], state_mlstm_c[l], state_mlstm_n[l],
                                    state_mlstm_m[l], state_lru_h[l], state_conv[l]),
                         n_batch=nbs, tp=tp_s, t_valid=dseq, prompt=False)
        p_states.append(stp)
        s_states.append(sts)
    fg = final_norm.astype(F32).reshape(1, d)
    y_prompt = _final_norm(hp.reshape(nb, tp_p, d), fg, r0=N_META, n_out=seq)
    y_sample = _final_norm(hs.reshape(nbs, tp_s, d), fg, r0=0, n_out=dseq)
    pouts = [jnp.stack(t) for t in zip(*p_states)]
    souts = [jnp.stack(t) for t in zip(*s_states)]
    return (y_prompt, y_sample, *pouts, *souts)
```

```python
import functools

import numpy as np
import jax
import jax.numpy as jnp
from jax import lax
from jax.experimental import pallas as pl
from jax.experimental.pallas import tpu as pltpu

F32 = jnp.float32
BF16 = jnp.bfloat16
I32 = jnp.int32

CHUNK = 64
N_META = 16
N_HEADS = 8
N_KV_HEADS = 2
HEAD_DIM = 64
N_IDX_HEADS = 4
IDX_DIM = 64
TOP_K_MAX = 256
M_HEADS = 4
M_QK = 64
M_V = 128
LRU_W = 512
LRU_BLOCKS = 8
CONV_W = 4
LRU_C = 8.0
N_BRANCH = 3
BRANCH_W = 512
EPS = 1e-6
NEG = -1e30

ATT_W = N_HEADS * HEAD_DIM
KV_W = N_KV_HEADS * HEAD_DIM
GROUP = N_HEADS // N_KV_HEADS

LANES = 128
ROW_PAD = 128
VMEM_LIMIT = 56 * 1024 * 1024
DSA_TILES_PER_GROUP = 2
MLSTM_SEQS = 2

SM_IK = 0
SM_IW = IDX_DIM
SM_MI = SM_IW + N_IDX_HEADS
SM_MF = SM_MI + M_HEADS
SM_W = 128


def _z_layout(d_model):
    widths = [("gt", N_BRANCH * d_model), ("aq", ATT_W), ("mv", M_HEADS * M_V), ("mo", M_HEADS * M_V),
              ("lx", LRU_W), ("lg", LRU_W), ("iq", N_IDX_HEADS * IDX_DIM), ("mq", M_HEADS * M_QK),
              ("mk", M_HEADS * M_QK), ("ak", KV_W), ("av", KV_W), ("sm", SM_W)]
    off, acc = {}, 0
    for name, w in widths:
        assert acc % w == 0, (name, acc, w)
        off[name] = (acc, w)
        acc += w
    return off, acc


def _pack_w_in(w_in, d_model, n_col_tiles):
    sizes = (ATT_W, KV_W, KV_W, N_IDX_HEADS * IDX_DIM, IDX_DIM, N_IDX_HEADS,
             M_HEADS * M_QK, M_HEADS * M_QK, M_HEADS * M_V, M_HEADS, M_HEADS, M_HEADS * M_V,
             LRU_W, LRU_W, N_BRANCH * d_model)
    names = ("aq", "ak", "av", "iq", "ik", "iw", "mq", "mk", "mv", "mi", "mf", "mo", "lx", "lg", "gt")
    assert w_in.shape[-1] == sum(sizes)
    parts, acc = {}, 0
    for n, s in zip(names, sizes):
        parts[n] = w_in[..., acc:acc + s]
        acc += s
    lead = w_in.shape[:-1]
    small = jnp.concatenate([parts["ik"], parts["iw"], parts["mi"], parts["mf"],
                             jnp.zeros(lead + (SM_W - SM_MF - M_HEADS,), w_in.dtype)], axis=-1)
    parts["sm"] = small
    off, zw = _z_layout(d_model)
    order = sorted(off, key=lambda n: off[n][0])
    cols = [parts[n] for n in order]
    zw_pad = -(-zw // (n_col_tiles * LANES)) * (n_col_tiles * LANES)
    if zw_pad > zw:
        cols.append(jnp.zeros(lead + (zw_pad - zw,), w_in.dtype))
    return jnp.concatenate(cols, axis=-1).astype(BF16), off, zw_pad


def _params(*sem):
    return pltpu.CompilerParams(dimension_semantics=sem, vmem_limit_bytes=VMEM_LIMIT)


def _bdot(a, b):
    return jnp.dot(a.astype(BF16), b.astype(BF16), preferred_element_type=F32)


def _bdot_nt(a, b):
    return lax.dot_general(a.astype(BF16), b.astype(BF16), (((1,), (1,)), ((), ())),
                           preferred_element_type=F32)


def _split3(a):
    a1 = a.astype(BF16)
    r1 = a - a1.astype(F32)
    a2 = r1.astype(BF16)
    a3 = (r1 - a2.astype(F32)).astype(BF16)
    return a1, a2, a3


def _dot3_nt(a, b):
    a1, a2, _ = _split3(a)
    b1, b2, _ = _split3(b)
    dn = (((1,), (1,)), ((), ()))
    f = lambda x, y: lax.dot_general(x, y, dn, preferred_element_type=F32)
    return f(a1, b1) + (f(a1, b2) + f(a2, b1))


def _dot3(a, b):
    a1, a2, _ = _split3(a)
    b1, b2, _ = _split3(b)
    f = lambda x, y: jnp.dot(x, y, preferred_element_type=F32)
    return f(a1, b1) + (f(a1, b2) + f(a2, b1))


def _rms_rows(x, g):
    return x * lax.rsqrt(jnp.mean(x * x, axis=-1, keepdims=True) + EPS) * g


def _sigmoid(x):
    return 1.0 / (1.0 + jnp.exp(-x))


def _log_sigmoid(x):
    return jnp.minimum(x, 0.0) - jnp.log1p(jnp.exp(-jnp.abs(x)))


def _softplus(x):
    return jnp.maximum(x, 0.0) + jnp.log1p(jnp.exp(-jnp.abs(x)))


def _gelu_tanh(x):
    c = np.float32(np.sqrt(2.0 / np.pi))
    return 0.5 * x * (1.0 + jnp.tanh(c * (x + np.float32(0.044715) * (x * x * x))))


def _inproj_kernel(x_ref, g_ref, w_ref, z_ref):
    hn = _rms_rows(x_ref[...], g_ref[...])
    z_ref[...] = jnp.dot(hn.astype(BF16), w_ref[...], preferred_element_type=F32)


def _inproj(x, g, w_packed, layer, *, tm, n_col_tiles):
    rows, d = x.shape
    zw = w_packed.shape[-1]
    tn = zw // n_col_tiles
    return pl.pallas_call(
        _inproj_kernel,
        grid=(n_col_tiles, rows // tm),
        in_specs=[pl.BlockSpec((tm, d), lambda j, i: (i, 0)),
                  pl.BlockSpec((None, 1, d), lambda j, i: (layer, 0, 0)),
                  pl.BlockSpec((None, d, tn), lambda j, i: (layer, 0, j))],
        out_specs=pl.BlockSpec((tm, tn), lambda j, i: (i, j)),
        out_shape=jax.ShapeDtypeStruct((rows, zw), F32),
        compiler_params=_params("parallel", "parallel"),
        name="inproj",
    )(x, g, w_packed)


def _kvprep_kernel(ak_ref, av_ref, sm_ref, kn_ref, ikn_ref, k_ref, v_ref, ki_ref, kb_ref, vt_ref, kic_ref,
                   *, t_valid):
    ak = ak_ref[...]
    kn = kn_ref[...]
    ks = [_rms_rows(ak[:, h * HEAD_DIM:(h + 1) * HEAD_DIM], kn) for h in range(N_KV_HEADS)]
    k_ref[...] = jnp.concatenate(ks, axis=-1)[:t_valid]
    av = av_ref[...]
    v_ref[...] = av[:t_valid]
    ki = _rms_rows(sm_ref[...][:, SM_IK:SM_IK + IDX_DIM], ikn_ref[...])
    ki_ref[...] = ki[:t_valid]
    for g in range(N_KV_HEADS):
        kb_ref[g] = ks[g].astype(BF16)
    vt_ref[...] = av.T.astype(BF16)
    hi = ki.astype(BF16).astype(F32)
    kic_ref[...] = jnp.concatenate([hi, ki - hi, hi, jnp.zeros_like(hi)], axis=-1).astype(BF16)


def _kvprep(z, off, k_norm, idx_k_norm, layer, *, n_batch, tp, t_valid):
    rows = z.shape[0]
    cb = lambda name: off[name][0] // off[name][1]
    return pl.pallas_call(
        functools.partial(_kvprep_kernel, t_valid=t_valid),
        grid=(n_batch,),
        in_specs=[pl.BlockSpec((tp, KV_W), lambda b: (b, cb("ak"))),
                  pl.BlockSpec((tp, KV_W), lambda b: (b, cb("av"))),
                  pl.BlockSpec((tp, SM_W), lambda b: (b, cb("sm"))),
                  pl.BlockSpec((None, 1, HEAD_DIM), lambda b: (layer, 0, 0)),
                  pl.BlockSpec((None, 1, IDX_DIM), lambda b: (layer, 0, 0))],
        out_specs=[pl.BlockSpec((None, t_valid, KV_W), lambda b: (b, 0, 0)),
                   pl.BlockSpec((None, t_valid, KV_W), lambda b: (b, 0, 0)),
                   pl.BlockSpec((None, t_valid, IDX_DIM), lambda b: (b, 0, 0)),
                   pl.BlockSpec((None, N_KV_HEADS, tp, HEAD_DIM), lambda b: (b, 0, 0, 0)),
                   pl.BlockSpec((None, KV_W, tp), lambda b: (b, 0, 0)),
                   pl.BlockSpec((None, tp, 4 * IDX_DIM), lambda b: (b, 0, 0))],
        out_shape=[jax.ShapeDtypeStruct((n_batch, t_valid, KV_W), F32),
                   jax.ShapeDtypeStruct((n_batch, t_valid, KV_W), F32),
                   jax.ShapeDtypeStruct((n_batch, t_valid, IDX_DIM), F32),
                   jax.ShapeDtypeStruct((n_batch, N_KV_HEADS, tp, HEAD_DIM), BF16),
                   jax.ShapeDtypeStruct((n_batch, KV_W, tp), BF16),
                   jax.ShapeDtypeStruct((n_batch, tp, 4 * IDX_DIM), BF16)],
        compiler_params=_params("parallel"),
        name="kvprep",
    )(z, z, z, k_norm, idx_k_norm)


_INT_MIN = np.int32(-2 ** 31)


def _sort_key_of(x):
    b = np.array(x, np.float32).view(np.int32)
    return np.int32(b ^ 0x7FFFFFFF) if b < 0 else np.int32(b)


_OK_KEY = _sort_key_of(0.5 * NEG)


def _dsa_kernel(aq_ref, iq_ref, sm_ref, k_ref, v_ref, ki_ref, qn_ref, o_ref, key_ref, msk_ref,
                *, tq, lp, l_valid, top_k, prompt):
    nblk = lp // LANES
    kf = np.float32(top_k)

    iq = iq_ref[...]
    sm = sm_ref[...]
    ki = ki_ref[...]
    sc = None
    for h in range(N_IDX_HEADS):
        s = _dot3_nt(iq[:, h * IDX_DIM:(h + 1) * IDX_DIM], ki)
        s = jnp.maximum(s * np.float32(IDX_DIM ** -0.5), 0.0)
        w = sm[:, SM_IW + h:SM_IW + h + 1] * np.float32(N_IDX_HEADS ** -0.5)
        sc = s * w if sc is None else sc + s * w
    sc = jnp.where(sc == 0.0, 0.0, sc)

    kpos = lax.broadcasted_iota(I32, (1, lp), 1)
    if prompt:
        qpos = pl.program_id(1) * tq + lax.broadcasted_iota(I32, (tq, 1), 0)
        allowed = ((kpos + (CHUNK - N_META)) // CHUNK <= (qpos + (CHUNK - N_META)) // CHUNK) & (kpos < l_valid)
    else:
        allowed = jnp.broadcast_to(kpos < l_valid, (tq, lp))
    scm = jnp.where(allowed, sc, np.float32(NEG))
    bits = pltpu.bitcast(scm, I32)
    key_ref[...] = jnp.where(bits < 0, bits ^ np.int32(0x7FFFFFFF), bits)

    def count_ge(cand):
        acc = jnp.zeros((tq, LANES), F32)
        for j in range(nblk):
            acc = acc + jnp.where(key_ref[:, j * LANES:(j + 1) * LANES] >= cand, 1.0, 0.0)
        return jnp.sum(acc, axis=-1, keepdims=True)

    ans = jnp.where(count_ge(jnp.zeros((tq, 1), I32)) >= kf, np.int32(0), _INT_MIN)

    def bit_step(t, ans):
        cand = ans | lax.shift_left(np.int32(1), np.int32(30) - t)
        return jnp.where(count_ge(cand) >= kf, cand, ans)

    thr = lax.fori_loop(0, 31, bit_step, ans)

    acc = jnp.zeros((tq, LANES), F32)
    for j in range(nblk):
        acc = acc + jnp.where(key_ref[:, j * LANES:(j + 1) * LANES] > thr, 1.0, 0.0)
    need = kf - jnp.sum(acc, axis=-1, keepdims=True)
    tri = (lax.broadcasted_iota(I32, (LANES, LANES), 0) < lax.broadcasted_iota(I32, (LANES, LANES), 1))
    tri = jnp.where(tri, 1.0, 0.0).astype(BF16)
    seen = jnp.zeros((tq, 1), F32)
    for j in range(nblk):
        kj = key_ref[:, j * LANES:(j + 1) * LANES]
        eq = jnp.where(kj == thr, 1.0, 0.0)
        before = seen + jnp.dot(eq.astype(BF16), tri, preferred_element_type=F32)
        sel = (kj > thr) | ((kj == thr) & (before < need))
        sel = sel & (kj > _OK_KEY)
        msk_ref[:, j * LANES:(j + 1) * LANES] = jnp.where(sel, 1.0, 0.0)
        seen = seen + jnp.sum(eq, axis=-1, keepdims=True)

    q = aq_ref[...]
    qn = qn_ref[...]
    kk = k_ref[...]
    vv = v_ref[...]
    sel = msk_ref[...] > 0.5
    outs = []
    for g in range(N_KV_HEADS):
        kg = kk[:, g * HEAD_DIM:(g + 1) * HEAD_DIM].astype(BF16)
        vg = vv[:, g * HEAD_DIM:(g + 1) * HEAD_DIM].astype(BF16)
        for hh in range(GROUP):
            h = g * GROUP + hh
            qh = _rms_rows(q[:, h * HEAD_DIM:(h + 1) * HEAD_DIM], qn)
            s = _bdot_nt(qh, kg) * np.float32(HEAD_DIM ** -0.5)
            s = jnp.where(sel, s, np.float32(NEG))
            p = jnp.exp(s - jnp.max(s, axis=-1, keepdims=True))
            den = jnp.sum(p, axis=-1, keepdims=True)
            outs.append(jnp.dot(p.astype(BF16), vg, preferred_element_type=F32) / den)
    o_ref[...] = jnp.concatenate(outs, axis=-1)


def _dsa(z, off, kmat, vmat, kimat, q_norm, layer, *, n_batch, rows_per_batch, tq, n_q, l_valid, top_k, prompt):
    lp = kmat.shape[1]
    assert lp % LANES == 0 and rows_per_batch % tq == 0
    rb = rows_per_batch // tq
    cb = lambda name: off[name][0] // off[name][1]
    kern = functools.partial(_dsa_kernel, tq=tq, lp=lp, l_valid=l_valid, top_k=top_k, prompt=prompt)
    return pl.pallas_call(
        kern,
        grid=(n_batch, n_q),
        in_specs=[pl.BlockSpec((tq, ATT_W), lambda b, i: (b * rb + i, cb("aq"))),
                  pl.BlockSpec((tq, N_IDX_HEADS * IDX_DIM), lambda b, i: (b * rb + i, cb("iq"))),
                  pl.BlockSpec((tq, SM_W), lambda b, i: (b * rb + i, cb("sm"))),
                  pl.BlockSpec((None, lp, KV_W), lambda b, i: (b, 0, 0)),
                  pl.BlockSpec((None, lp, KV_W), lambda b, i: (b, 0, 0)),
                  pl.BlockSpec((None, lp, IDX_DIM), lambda b, i: (b, 0, 0)),
                  pl.BlockSpec((None, 1, HEAD_DIM), lambda b, i: (layer, 0, 0))],
        out_specs=pl.BlockSpec((tq, ATT_W), lambda b, i: (b * n_q + i, 0)),
        out_shape=jax.ShapeDtypeStruct((n_batch * n_q * tq, ATT_W), F32),
        scratch_shapes=[pltpu.VMEM((tq, lp), I32), pltpu.VMEM((tq, lp), F32)],
        compiler_params=_params("parallel", "parallel"),
        name="dsa",
    )(z, z, z, kmat, vmat, kimat, q_norm)


def _row_blocks(x):
    return [x[j * LANES:(j + 1) * LANES, :] for j in range(x.shape[0] // LANES)]


def _col_reduce(blocks, op, red):
    a = blocks[0]
    for blk in blocks[1:]:
        a = op(a, blk)
    r = a.shape[0]
    while r > 8:
        r //= 2
        a = op(a[:r], a[r:])
    return red(a, axis=0, keepdims=True)


_WORD = 32
_SUB = 8


def _bit_planes(slabs):
    x = list(slabs)
    for d, m in ((16, 0x0000FFFF), (8, 0x00FF00FF), (4, 0x0F0F0F0F), (2, 0x33333333), (1, 0x55555555)):
        for i in range(_WORD):
            if not i & d:
                a, b = x[i], x[i + d]
                t = (lax.shift_right_logical(a, np.int32(d)) ^ b) & np.int32(m)
                x[i + d] = b ^ t
                x[i] = a ^ lax.shift_left(t, np.int32(d))
    return x


def _kth_largest(key_ref, lp, tq, top_k):
    n_pieces = lp // _SUB
    n_grp = -(-n_pieces // _WORD)
    slabs = []
    for i in range(_WORD):
        parts = []
        for g in range(n_grp):
            r0 = (g * _WORD + i) * _SUB
            parts.append(key_ref[r0:r0 + _SUB, :] ^ _INT_MIN if r0 < lp else jnp.zeros((_SUB, tq), I32))
        slabs.append(jnp.concatenate(parts, axis=0) if n_grp > 1 else parts[0])
    planes = _bit_planes(slabs)
    grp = lax.broadcasted_iota(I32, (n_grp * _SUB, tq), 0) // _SUB
    last = n_pieces - (n_grp - 1) * _WORD
    last_mask = np.int32(-1) if last == _WORD else np.int32((1 << last) - 1)
    act = jnp.where(grp < n_grp - 1, np.int32(-1), last_mask)
    above = jnp.zeros((1, tq), I32)
    thr_u = jnp.zeros((1, tq), I32)
    for b in range(_WORD - 1, -1, -1):
        ones = act & planes[b]
        n1 = jnp.sum(lax.population_count(ones), axis=0, keepdims=True)
        take = above + n1 >= top_k
        act = jnp.where(take, ones, act ^ ones)
        above = jnp.where(take, above, above + n1)
        thr_u = jnp.where(take, thr_u | (_INT_MIN if b == _WORD - 1 else np.int32(1 << b)), thr_u)
    n_eq = jnp.sum(lax.population_count(act), axis=0, keepdims=True)
    return thr_u ^ _INT_MIN, above + n_eq


def _order_key(bits):
    return jnp.where(bits < 0, _INT_MIN - bits, bits)


_OK_KEY_T = np.int32(-int(np.array(-0.5 * NEG, np.float32).view(np.int32)))


def _dsa_t_kernel(aq_ref, iq_ref, sm_ref, kb_ref, vt_ref, kic_ref, qn_ref, *rest, tq, lp, l_valid, top_k, q0):
    o_ref, key_ref, bias_ref = rest[-3:]
    kf = np.float32(top_k)
    nt = (((1,), (1,)), ((), ()))

    smt = sm_ref[...].T
    iq = iq_ref[...]
    pieces = []
    for h in range(N_IDX_HEADS):
        qh = iq[:, h * IDX_DIM:(h + 1) * IDX_DIM]
        hi = qh.astype(BF16).astype(F32)
        pieces.append(jnp.concatenate([hi, hi, qh - hi, jnp.zeros_like(hi)], axis=-1))
    qcat = jnp.concatenate(pieces, axis=0).astype(BF16)
    st = lax.dot_general(kic_ref[...], qcat, nt, preferred_element_type=F32)
    sc = None
    for h in range(N_IDX_HEADS):
        w = smt[SM_IW + h:SM_IW + h + 1, :] * np.float32(N_IDX_HEADS ** -0.5 * IDX_DIM ** -0.5)
        t = jnp.maximum(st[:, h * tq:(h + 1) * tq], 0.0) * w
        sc = t if sc is None else sc + t

    kpos = lax.broadcasted_iota(I32, (lp, 1), 0)
    qpos = (q0 + pl.program_id(1)) * tq + lax.broadcasted_iota(I32, (1, tq), 1)
    allowed = ((kpos + (CHUNK - N_META)) // CHUNK <= (qpos + (CHUNK - N_META)) // CHUNK) & (kpos < l_valid)
    scm = jnp.where(allowed, sc, np.float32(NEG))
    key_ref[...] = _order_key(pltpu.bitcast(scm, I32))

    thr, n_ge = _kth_largest(key_ref, lp, tq, top_k)
    exact = (jnp.min(jnp.where(n_ge == top_k, 1.0, 0.0)) > 0.5).astype(I32)

    @pl.when(exact == 1)
    def _():
        key = key_ref[...]
        bias_ref[...] = jnp.where((key >= thr) & (key > _OK_KEY_T), 0.0, np.float32(NEG))

    @pl.when(exact == 0)
    def _():
        need = kf - _col_reduce([jnp.where(key_ref[j * LANES:(j + 1) * LANES, :] > thr, 1.0, 0.0)
                                 for j in range(lp // LANES)], jnp.add, jnp.sum)
        tri = (lax.broadcasted_iota(I32, (LANES, LANES), 1) < lax.broadcasted_iota(I32, (LANES, LANES), 0))
        tri = jnp.where(tri, 1.0, 0.0).astype(BF16)
        seen = jnp.zeros((1, tq), F32)
        for j in range(lp // LANES):
            kj = key_ref[j * LANES:(j + 1) * LANES, :]
            eq = jnp.where(kj == thr, 1.0, 0.0)
            before = seen + jnp.dot(tri, eq.astype(BF16), preferred_element_type=F32)
            sel = ((kj > thr) | ((kj == thr) & (before < need))) & (kj > _OK_KEY_T)
            bias_ref[j * LANES:(j + 1) * LANES, :] = jnp.where(sel, 0.0, np.float32(NEG))
            seen = seen + jnp.sum(eq, axis=0, keepdims=True)

    q = aq_ref[...]
    qn = qn_ref[...] * np.float32(HEAD_DIM ** -0.5 * np.log2(np.e))
    bias = bias_ref[...]
    outs = []
    for g in range(N_KV_HEADS):
        qs = [_rms_rows(q[:, h * HEAD_DIM:(h + 1) * HEAD_DIM], qn) for h in range(g * GROUP, (g + 1) * GROUP)]
        qs = jnp.concatenate(qs, axis=0).astype(BF16)
        s_all = lax.dot_general(kb_ref[g], qs, nt, preferred_element_type=F32)
        vtg = vt_ref[g * HEAD_DIM:(g + 1) * HEAD_DIM, :]
        for hh in range(GROUP):
            s = s_all[:, hh * tq:(hh + 1) * tq] + bias
            p = jnp.exp2(s - _col_reduce(_row_blocks(s), jnp.maximum, jnp.max))
            den = _col_reduce(_row_blocks(p), jnp.add, jnp.sum)
            outs.append(jnp.dot(vtg, p.astype(BF16), preferred_element_type=F32) / den)
    o_ref[...] = jnp.concatenate(outs, axis=0).T


def _dsa_t(z, off, kb, vt, kic, q_norm, layer, out_buf, *, n_batch, tp, tq, q0, n_q, lp, l_valid, top_k):
    prev = [] if out_buf is None else [out_buf]
    rb = tp // tq
    cb = lambda name: off[name][0] // off[name][1]
    kern = functools.partial(_dsa_t_kernel, tq=tq, lp=lp, l_valid=l_valid, top_k=top_k, q0=q0)
    qrow = lambda b, i: b * rb + q0 + i
    return pl.pallas_call(
        kern,
        grid=(n_batch, n_q),
        in_specs=[pl.BlockSpec((tq, ATT_W), lambda b, i: (qrow(b, i), cb("aq"))),
                  pl.BlockSpec((tq, N_IDX_HEADS * IDX_DIM), lambda b, i: (qrow(b, i), cb("iq"))),
                  pl.BlockSpec((tq, SM_W), lambda b, i: (qrow(b, i), cb("sm"))),
                  pl.BlockSpec((None, N_KV_HEADS, lp, HEAD_DIM), lambda b, i: (b, 0, 0, 0)),
                  pl.BlockSpec((None, KV_W, lp), lambda b, i: (b, 0, 0)),
                  pl.BlockSpec((None, lp, 4 * IDX_DIM), lambda b, i: (b, 0, 0)),
                  pl.BlockSpec((None, 1, HEAD_DIM), lambda b, i: (layer, 0, 0))]
                 + [pl.BlockSpec(memory_space=pl.ANY) for _ in prev],
        out_specs=pl.BlockSpec((None, tq, ATT_W), lambda b, i: (b, q0 + i, 0)),
        out_shape=jax.ShapeDtypeStruct((n_batch, tp, ATT_W), F32),
        input_output_aliases={7: 0} if prev else {},
        scratch_shapes=[pltpu.VMEM((lp, tq), I32), pltpu.VMEM((lp, tq), F32)],
        compiler_params=_params("parallel", "parallel"),
        name="dsa_t",
    )(z, z, z, kb, vt, kic, q_norm, *prev)


def _mlstm_kernel(bi_ref, bf_ref, mq_ref, mk_ref, mv_ref, mo_ref, sm_ref, gn_ref, c0_ref, n0_ref, m0_ref,
                  yb_ref, c1_ref, n1_ref, m1_ref, c_s, n_s, m_s, *, lc, t_valid, layer, nbb):
    j = pl.program_id(1)

    @pl.when(j == 0)
    def _():
        c_s[...] = c0_ref[...]
        n_s[...] = n0_ref[...]
        m_s[...] = m0_ref[...]

    s_io = lax.broadcasted_iota(I32, (lc, lc), 0)
    t_io = lax.broadcasted_iota(I32, (lc, lc), 1)
    causal = s_io <= t_io
    tri = jnp.where(t_io <= s_io, 1.0, 0.0).astype(BF16)
    tri3 = jnp.concatenate([tri, tri, tri], axis=1)
    pos_col = j * lc + lax.broadcasted_iota(I32, (lc, 1), 0)
    pos_row = j * lc + lax.broadcasted_iota(I32, (1, lc), 1)
    lane = lax.broadcasted_iota(I32, (1, LANES), 1)
    nt = (((1,), (1,)), ((), ()))
    c_in = [[c_s[bb, h] for h in range(M_HEADS)] for bb in range(nbb)]
    n_in = [[n_s[bb, h] for h in range(M_HEADS)] for bb in range(nbb)]
    m_in = [m_s[bb] for bb in range(nbb)]
    c_out, n_out, m_outs, y_out = [], [], [], []
    for bb in range(nbb):
        sm = sm_ref[bb]
        smt = sm.T
        mq = mq_ref[bb]
        mk = mk_ref[bb]
        mv = mv_ref[bb]
        mo = mo_ref[bb]
        m_all = m_in[bb]
        m_out = m_all
        for h in range(M_HEADS):
            bi = bi_ref[layer, h]
            bf = bf_ref[layer, h]
            i_col = jnp.where(pos_col < t_valid, sm[:, SM_MI + h:SM_MI + h + 1] + bi, np.float32(NEG))
            f_col = jnp.where(pos_col < t_valid, _log_sigmoid(sm[:, SM_MF + h:SM_MF + h + 1] + bf), 0.0)
            i_row = jnp.where(pos_row < t_valid, smt[SM_MI + h:SM_MI + h + 1, :] + bi, np.float32(NEG))
            f_b = jnp.broadcast_to(f_col, (lc, lc))
            b_row = jnp.sum(jnp.where(causal, f_b, 0.0), axis=0, keepdims=True)
            p1, p2, p3 = _split3(f_b)
            b_src = jnp.dot(tri3, jnp.concatenate([p1, p2, p3], axis=0), preferred_element_type=F32)
            m_prev = m_all[:, h:h + 1]
            log_d = jnp.where(causal, b_row - b_src + i_col, np.float32(NEG))
            m_t = jnp.maximum(b_row + m_prev, jnp.max(log_d, axis=0, keepdims=True))
            d = jnp.exp(log_d - m_t)
            inter = jnp.exp(b_row + m_prev - m_t)
            qh = mq[:, h * M_QK:(h + 1) * M_QK].astype(BF16)
            kh = (mk[:, h * M_QK:(h + 1) * M_QK] * np.float32(M_QK ** -0.5)).astype(BF16)
            vt = mv[:, h * M_V:(h + 1) * M_V].T
            c_h = c_in[bb][h]
            n_h = n_in[bb][h]
            w = lax.dot_general(kh, qh, nt, preferred_element_type=F32) * d
            cn = jnp.concatenate([c_h, jnp.broadcast_to(n_h, (_SUB, M_QK))], axis=0).astype(BF16)
            cq = lax.dot_general(cn, qh, nt, preferred_element_type=F32)
            num = jnp.dot(vt.astype(BF16), w.astype(BF16), preferred_element_type=F32) + inter * cq[:M_V]
            den = jnp.sum(w, axis=0, keepdims=True) + inter * cq[M_V:M_V + 1]
            hm = num / jnp.maximum(jnp.abs(den), jnp.exp(-m_t))
            m_new = m_t[:, lc - 1:lc]
            b_last = b_row[:, lc - 1:lc]
            decay = jnp.exp(b_last + m_prev - m_new)
            ws = jnp.exp(b_last - b_row + i_row - m_new)
            ws_hi = ws.astype(BF16).astype(F32)
            upd = jnp.concatenate([vt * ws, ws_hi, ws - ws_hi, jnp.zeros((_SUB - 2, lc), F32)], axis=0)
            upd = jnp.dot(upd.astype(BF16), kh, preferred_element_type=F32)
            c_out.append(decay * c_h + upd[:M_V])
            n_out.append(decay * n_h + upd[M_V:M_V + 1] + upd[M_V + 1:M_V + 2])
            m_out = jnp.where(lane == h, m_new, m_out)
            y = hm * lax.rsqrt(jnp.mean(hm * hm, axis=0, keepdims=True) + EPS) * gn_ref[h * M_V:(h + 1) * M_V, :]
            y_out.append(y.T * _sigmoid(mo[:, h * M_V:(h + 1) * M_V]))
        m_outs.append(m_out)
    for bb in range(nbb):
        for h in range(M_HEADS):
            c_s[bb, h] = c_out[bb * M_HEADS + h]
            n_s[bb, h] = n_out[bb * M_HEADS + h]
            yb_ref[bb, :, h * M_V:(h + 1) * M_V] = y_out[bb * M_HEADS + h]
        m_s[bb] = m_outs[bb]

    @pl.when(j == pl.num_programs(1) - 1)
    def _():
        c1_ref[...] = c_s[...]
        n1_ref[...] = n_s[...]
        m1_ref[...] = m_s[...]


def _mlstm(z, off, bias_i, bias_f, m_norm, c0, n0, m0, layer, *, n_batch, tp, t_valid, lc, nbb):
    assert n_batch % nbb == 0 and lc == LANES
    nc = tp // lc
    z3 = z.reshape(n_batch, tp, z.shape[-1])
    cb = lambda name: off[name][0] // off[name][1]
    row = lambda name: pl.BlockSpec((nbb, lc, off[name][1]), lambda b, j: (b, j, cb(name)))
    smem = pl.BlockSpec(memory_space=pltpu.SMEM)
    st_c = pl.BlockSpec((nbb, M_HEADS, M_V, M_QK), lambda b, j: (b, 0, 0, 0))
    st_n = pl.BlockSpec((nbb, M_HEADS, 1, M_QK), lambda b, j: (b, 0, 0, 0))
    st_m = pl.BlockSpec((nbb, 1, LANES), lambda b, j: (b, 0, 0))
    kern = functools.partial(_mlstm_kernel, lc=lc, t_valid=t_valid, layer=layer, nbb=nbb)
    yb, c1, n1, m1 = pl.pallas_call(
        kern,
        grid=(n_batch // nbb, nc),
        in_specs=[smem, smem, row("mq"), row("mk"), row("mv"), row("mo"), row("sm"),
                  pl.BlockSpec((None, M_HEADS * M_V, LANES), lambda b, j: (layer, 0, 0)),
                  st_c, st_n, st_m],
        out_specs=[pl.BlockSpec((nbb, lc, M_HEADS * M_V), lambda b, j: (b, j, 0)), st_c, st_n, st_m],
        out_shape=[jax.ShapeDtypeStruct((n_batch, tp, M_HEADS * M_V), F32),
                   jax.ShapeDtypeStruct(c0.shape, F32), jax.ShapeDtypeStruct(n0.shape, F32),
                   jax.ShapeDtypeStruct(m0.shape, F32)],
        scratch_shapes=[pltpu.VMEM((nbb, M_HEADS, M_V, M_QK), F32), pltpu.VMEM((nbb, M_HEADS, 1, M_QK), F32),
                        pltpu.VMEM((nbb, 1, LANES), F32)],
        compiler_params=_params("parallel", "arbitrary"),
        name="mlstm",
    )(bias_i, bias_f, z3, z3, z3, z3, z3, m_norm, c0, n0, m0)
    return yb.reshape(n_batch * tp, M_HEADS * M_V), c1, n1, m1


def _shift_rows_in_group(x, s, fill):
    rolled = pltpu.roll(x, s, 0)
    row = lax.broadcasted_iota(I32, x.shape, 0)
    return jnp.where((row & 7) >= s, rolled, fill)


def _lru_kernel(lx_ref, lg_ref, cw_ref, cb_ref, w3_ref, ba_ref, bx_ref, lam_ref, h0_ref, buf0_ref,
                yc_ref, h1_ref, buf1_ref, prev_s, h_s, *, tt, t_valid):
    j = pl.program_id(1)

    @pl.when(j == 0)
    def _():
        prev_s[...] = buf0_ref[...]
        h_s[...] = h0_ref[...]

    lx = lx_ref[...]
    cw = cw_ref[...]
    prev = prev_s[...]
    row8 = lax.broadcasted_iota(I32, prev.shape, 0)
    xc = cb_ref[...] + lx * cw[CONV_W - 1:CONV_W, :]
    for s in range(1, CONV_W):
        rolled = pltpu.roll(lx, s, 0)
        head = jnp.where(row8 < s, pltpu.roll(prev, s, 0), rolled[:8])
        shifted = jnp.concatenate([head, rolled[8:]], axis=0) if tt > 8 else head
        xc = xc + shifted * cw[CONV_W - 1 - s:CONV_W - s, :]
    prev_s[...] = lx[tt - 8:, :]

    x_hi = xc.astype(BF16)
    x_lo = (xc - x_hi.astype(F32)).astype(BF16)
    gates = jnp.dot(jnp.concatenate([x_hi, x_hi, x_lo], axis=1), w3_ref[...], preferred_element_type=F32)
    r = _sigmoid(gates[:, :LRU_W] + ba_ref[...])
    gi = _sigmoid(gates[:, LRU_W:] + bx_ref[...])
    log_a = np.float32(-LRU_C) * r * _softplus(-lam_ref[...])
    a = jnp.exp(log_a)
    u = jnp.sqrt(jnp.tanh(-log_a) * (a * a + 1.0)) * gi * xc
    s = 1
    while s < 8:
        u = a * _shift_rows_in_group(u, s, 0.0) + u
        a = a * _shift_rows_in_group(a, s, 1.0)
        s *= 2
    carry = h_s[...]
    hs = []
    for g in range(tt // 8):
        hg = a[g * 8:(g + 1) * 8, :] * carry + u[g * 8:(g + 1) * 8, :]
        carry = hg[7:8, :]
        hs.append(hg)
    h = jnp.concatenate(hs, axis=0)
    h_s[...] = carry
    yc_ref[...] = h * _gelu_tanh(lg_ref[...])

    last = t_valid - 1

    @pl.when(j == last // tt)
    def _():
        r0 = last % tt
        h1_ref[...] = h[r0:r0 + 1, :]
        buf1_ref[...] = lx[r0 - 7:r0 + 1, :]


def _lru(z, off, conv_w, conv_b, w3, ba, bx, lam, h0, buf0, layer, *, n_batch, tp, t_valid, tt):
    assert t_valid % 8 == 0 and tt % 8 == 0
    nt = tp // tt
    cb = lambda name: off[name][0] // off[name][1]
    row = lambda name: pl.BlockSpec((tt, LRU_W), lambda b, j: (b * nt + j, cb(name)))
    vec = lambda n: pl.BlockSpec((None, n, LRU_W), lambda b, j: (layer, 0, 0))
    st_h = pl.BlockSpec((None, 1, LRU_W), lambda b, j: (b, 0, 0))
    st_b = pl.BlockSpec((None, 8, LRU_W), lambda b, j: (b, 0, 0))
    kern = functools.partial(_lru_kernel, tt=tt, t_valid=t_valid)
    return pl.pallas_call(
        kern,
        grid=(n_batch, nt),
        in_specs=[row("lx"), row("lg"), vec(CONV_W), vec(1),
                  pl.BlockSpec((None, 3 * LRU_W, 2 * LRU_W), lambda b, j: (layer, 0, 0)), vec(1), vec(1), vec(1),
                  st_h, st_b],
        out_specs=[pl.BlockSpec((tt, LRU_W), lambda b, j: (b * nt + j, 0)), st_h, st_b],
        out_shape=[jax.ShapeDtypeStruct((n_batch * tp, LRU_W), F32),
                   jax.ShapeDtypeStruct(h0.shape, F32), jax.ShapeDtypeStruct(buf0.shape, F32)],
        scratch_shapes=[pltpu.VMEM((8, LRU_W), F32), pltpu.VMEM((1, LRU_W), F32)],
        compiler_params=_params("parallel", "arbitrary"),
        name="lru",
    )(z, z, conv_w, conv_b, w3, ba, bx, lam, h0, buf0)


def _merge_kernel(x_ref, ya_ref, yb_ref, yc_ref, gt_ref, wb_ref, wo_ref, o_ref, *, d):
    acc = None
    for b, y_ref in enumerate((ya_ref, yb_ref, yc_ref)):
        proj = jnp.dot(y_ref[...].astype(BF16), wb_ref[b], preferred_element_type=F32)
        t = _sigmoid(gt_ref[:, b * d:(b + 1) * d]) * proj
        acc = t if acc is None else acc + t
    o_ref[...] = x_ref[...] + jnp.dot(acc.astype(BF16), wo_ref[...], preferred_element_type=F32)


def _merge(x, ya, yb, yc, z, off, w_branch, w_out, layer, *, tm):
    rows, d = x.shape
    assert off["gt"][0] == 0
    rowspec = lambda w: pl.BlockSpec((tm, w), lambda i: (i, 0))
    return pl.pallas_call(
        functools.partial(_merge_kernel, d=d),
        grid=(rows // tm,),
        in_specs=[rowspec(d), rowspec(BRANCH_W), rowspec(BRANCH_W), rowspec(BRANCH_W), rowspec(N_BRANCH * d),
                  pl.BlockSpec((None, N_BRANCH, BRANCH_W, d), lambda i: (layer, 0, 0, 0)),
                  pl.BlockSpec((None, d, d), lambda i: (layer, 0, 0))],
        out_specs=rowspec(d),
        out_shape=jax.ShapeDtypeStruct((rows, d), F32),
        compiler_params=_params("parallel"),
        name="merge",
    )(x, ya, yb, yc, z, w_branch, w_out)


def _ffn_kernel(x_ref, g_ref, wu_ref, wd_ref, o_ref, hf_s, acc_s):
    j = pl.program_id(1)

    @pl.when(j == 0)
    def _():
        hf_s[...] = _rms_rows(x_ref[...], g_ref[...]).astype(BF16)
        acc_s[...] = jnp.zeros_like(acc_s)

    up = jnp.dot(hf_s[...], wu_ref[...], preferred_element_type=F32)
    act = jnp.square(jnp.maximum(up, 0.0))
    acc_s[...] += jnp.dot(act.astype(BF16), wd_ref[...], preferred_element_type=F32)

    @pl.when(j == pl.num_programs(1) - 1)
    def _():
        o_ref[...] = x_ref[...] + acc_s[...]


def _ffn(x, g, w_up, w_down, layer, *, tm, tf):
    rows, d = x.shape
    dff = w_up.shape[-1]
    return pl.pallas_call(
        _ffn_kernel,
        grid=(rows // tm, dff // tf),
        in_specs=[pl.BlockSpec((tm, d), lambda i, j: (i, 0)),
                  pl.BlockSpec((None, 1, d), lambda i, j: (layer, 0, 0)),
                  pl.BlockSpec((None, d, tf), lambda i, j: (layer, 0, j)),
                  pl.BlockSpec((None, tf, d), lambda i, j: (layer, j, 0))],
        out_specs=pl.BlockSpec((tm, d), lambda i, j: (i, 0)),
        out_shape=jax.ShapeDtypeStruct((rows, d), F32),
        scratch_shapes=[pltpu.VMEM((tm, d), BF16), pltpu.VMEM((tm, d), F32)],
        compiler_params=_params("parallel", "arbitrary"),
        name="ffn",
    )(x, g, w_up, w_down)


def _final_kernel(x_ref, g_ref, o_ref, *, r0, n_out):
    o_ref[...] = _rms_rows(x_ref[r0:r0 + n_out, :], g_ref[...])


def _final_norm(x3, g, *, r0, n_out):
    nb, tp, d = x3.shape
    return pl.pallas_call(
        functools.partial(_final_kernel, r0=r0, n_out=n_out),
        grid=(nb,),
        in_specs=[pl.BlockSpec((None, tp, d), lambda b: (b, 0, 0)), pl.BlockSpec((1, d), lambda b: (0, 0))],
        out_specs=pl.BlockSpec((None, n_out, d), lambda b: (b, 0, 0)),
        out_shape=jax.ShapeDtypeStruct((nb, n_out, d), F32),
        compiler_params=_params("parallel"),
        name="final_norm",
    )(x3, g)


def _pick_tile(n, pref):
    t = pref
    while n % t:
        t //= 2
    return t


def _layer(x, p, layer, st, *, n_batch, tp, t_valid, prompt):
    d = x.shape[1]
    rows = n_batch * tp
    off = p["off"]
    tm = _pick_tile(rows, 512)
    z = _inproj(x, p["norm_mix"], p["w_in"], layer, tm=tm, n_col_tiles=p["n_col_tiles"])
    k_new, v_new, ki_new, kb, vt, kic = _kvprep(z, off, p["k_norm"], p["idx_k_norm"], layer, n_batch=n_batch,
                                                tp=tp, t_valid=t_valid)

    if prompt:
        top_k = min(TOP_K_MAX, (t_valid - N_META) // 4)
        tq, n_q = ROW_PAD, tp // ROW_PAD
        ya3, i0 = None, 0
        while i0 < n_q:
            i1 = n_q if n_q - i0 <= DSA_TILES_PER_GROUP + 1 else i0 + DSA_TILES_PER_GROUP
            lp = min(tp, -(-(i1 * tq + CHUNK) // LANES) * LANES)
            assert lp >= top_k
            ya3 = _dsa_t(z, off, kb, vt, kic, p["q_norm"], layer, ya3, n_batch=n_batch, tp=tp, tq=tq, q0=i0,
                         n_q=i1 - i0, lp=lp, l_valid=t_valid, top_k=top_k)
            i0 = i1
        ya = ya3.reshape(rows, ATT_W)
    else:
        past = st[0].shape[1]
        l_valid = past + t_valid
        lp = -(-l_valid // LANES) * LANES
        cat = lambda old, new: jnp.concatenate(
            [old.reshape(n_batch, past, -1).astype(F32), new,
             jnp.zeros((n_batch, lp - l_valid, new.shape[-1]), F32)], axis=1)
        kmat, vmat, kimat = cat(st[0], k_new), cat(st[1], v_new), cat(st[2], ki_new)
        top_k = min(TOP_K_MAX, l_valid // 4)
        tq = t_valid
        ya = _dsa(z, off, kmat, vmat, kimat, p["q_norm"], layer, n_batch=n_batch, rows_per_batch=tp, tq=tq,
                  n_q=1, l_valid=l_valid, top_k=top_k, prompt=False)
        ya = jnp.pad(ya.reshape(n_batch, tq, ATT_W), ((0, 0), (0, tp - tq), (0, 0))).reshape(rows, ATT_W)

    if prompt:
        c0 = jnp.zeros((n_batch, M_HEADS, M_V, M_QK), F32)
        n0 = jnp.zeros((n_batch, M_HEADS, 1, M_QK), F32)
        m0 = jnp.zeros((n_batch, 1, LANES), F32)
        h0 = jnp.zeros((n_batch, 1, LRU_W), F32)
        buf0 = jnp.zeros((n_batch, 8, LRU_W), F32)
    else:
        c0 = st[3].astype(F32)
        n0 = st[4].astype(F32).reshape(n_batch, M_HEADS, 1, M_QK)
        m0 = jnp.pad(st[5].astype(F32), ((0, 0), (0, LANES - M_HEADS))).reshape(n_batch, 1, LANES)
        h0 = st[6].astype(F32).reshape(n_batch, 1, LRU_W)
        buf0 = jnp.pad(st[7].astype(F32), ((0, 0), (8 - (CONV_W - 1), 0), (0, 0)))
    yb, c1, n1, m1 = _mlstm(z, off, p["m_bias_i"], p["m_bias_f"], p["m_norm"], c0, n0, m0, layer,
                            n_batch=n_batch, tp=tp, t_valid=t_valid, lc=ROW_PAD, nbb=_pick_tile(n_batch, MLSTM_SEQS))
    yc, h1, buf1 = _lru(z, off, p["conv_w"], p["conv_b"], p["lru_w3"], p["lru_ba"], p["lru_bx"],
                        p["lru_lambda"], h0, buf0, layer, n_batch=n_batch, tp=tp, t_valid=t_valid, tt=ROW_PAD)

    x = _merge(x, ya, yb, yc, z, off, p["w_branch"], p["w_out"], layer, tm=tm)
    x = _ffn(x, p["norm_ffn"], p["w_up"], p["w_down"], layer, tm=tm, tf=_pick_tile(p["w_up"].shape[-1], 2048))
    state = (k_new.reshape(n_batch, t_valid, N_KV_HEADS, HEAD_DIM),
             v_new.reshape(n_batch, t_valid, N_KV_HEADS, HEAD_DIM), ki_new,
             c1, n1.reshape(n_batch, M_HEADS, M_QK), m1[:, 0, :M_HEADS],
             h1.reshape(n_batch, LRU_W), buf1[:, 8 - (CONV_W - 1):])
    return x, state


def _block_diag(w):
    depth, nb, c, _ = w.shape
    eye = jnp.eye(nb, dtype=w.dtype)
    return jnp.einsum("lncd,nm->lncmd", w, eye).reshape(depth, nb * c, nb * c)


def _gate_weights(wa, wx):
    w = jnp.concatenate([_block_diag(wa.astype(F32)), _block_diag(wx.astype(F32))], axis=-1)
    hi = w.astype(BF16)
    lo = (w - hi.astype(F32)).astype(BF16)
    return jnp.concatenate([hi, lo, hi], axis=1)


def kernel(x_prompt, x_sample, cache_k, cache_v, cache_idx_k, state_mlstm_c, state_mlstm_n, state_mlstm_m, state_lru_h, state_conv, meta_tokens, norm_mix, w_in, q_norm, k_norm, idx_k_norm, m_bias_i, m_bias_f, m_norm, conv_w, conv_b, lru_wa, lru_ba, lru_wx, lru_bx, lru_lambda, w_branch, w_out, norm_ffn, w_up, w_down, final_norm):
    depth = w_in.shape[0]
    nb, seq, d = x_prompt.shape
    nbs, dseq, _ = x_sample.shape
    n_col_tiles = 3
    w_packed, off, _ = _pack_w_in(w_in, d, n_col_tiles)
    vec3 = lambda a: a.astype(F32).reshape(depth, 1, -1)
    p = dict(off=off, n_col_tiles=n_col_tiles, w_in=w_packed,
             norm_mix=vec3(norm_mix), q_norm=vec3(q_norm), k_norm=vec3(k_norm), idx_k_norm=vec3(idx_k_norm),
             m_bias_i=m_bias_i.astype(F32), m_bias_f=m_bias_f.astype(F32), m_norm=jnp.broadcast_to(m_norm.astype(F32).reshape(depth, M_HEADS * M_V, 1), (depth, M_HEADS * M_V, LANES)),
             conv_w=conv_w.astype(F32), conv_b=vec3(conv_b),
             lru_w3=_gate_weights(lru_wa, lru_wx), lru_ba=vec3(lru_ba), lru_bx=vec3(lru_bx),
             lru_lambda=vec3(lru_lambda),
             w_branch=w_branch.astype(BF16), w_out=w_out.astype(BF16), norm_ffn=vec3(norm_ffn),
             w_up=w_up.astype(BF16), w_down=w_down.astype(BF16))

    t_p = N_META + seq
    tp_p = -(-t_p // ROW_PAD) * ROW_PAD
    meta = jnp.broadcast_to(meta_tokens[None].astype(x_prompt.dtype), (nb, N_META, d))
    hp = jnp.concatenate([meta, x_prompt, jnp.zeros((nb, tp_p - t_p, d), x_prompt.dtype)], axis=1)
    hp = hp.reshape(nb * tp_p, d)
    tp_s = -(-dseq // ROW_PAD) * ROW_PAD
    hs = jnp.pad(x_sample, ((0, 0), (0, tp_s - dseq), (0, 0))).reshape(nbs * tp_s, d)

    p_states, s_states = [], []
    for l in range(depth):
        hp, stp = _layer(hp, p, l, None, n_batch=nb, tp=tp_p, t_valid=t_p, prompt=True)
        hs, sts = _layer(hs, p, l, (cache_k[l], cache_v[l], cache_idx_k[l], state_mlstm_c[l], state_mlstm_n[l],
                                    state_mlstm_m[l], state_lru_h[l], state_conv[l]),
                         n_batch=nbs, tp=tp_s, t_valid=dseq, prompt=False)
        p_states.append(stp)
        s_states.append(sts)
    fg = final_norm.astype(F32).reshape(1, d)
    y_prompt = _final_norm(hp.reshape(nb, tp_p, d), fg, r0=N_META, n_out=seq)
    y_sample = _final_norm(hs.reshape(nbs, tp_s, d), fg, r0=0, n_out=dseq)
    pouts = [jnp.stack(t) for t in zip(*p_states)]
    souts = [jnp.stack(t) for t in zip(*s_states)]
    return (y_prompt, y_sample, *pouts, *souts)
```

```python
import functools

import numpy as np
import jax
import jax.numpy as jnp
from jax import lax
from jax.experimental import pallas as pl
from jax.experimental.pallas import tpu as pltpu

F32 = jnp.float32
BF16 = jnp.bfloat16
I32 = jnp.int32

CHUNK = 64
N_META = 16
N_HEADS = 8
N_KV_HEADS = 2
HEAD_DIM = 64
N_IDX_HEADS = 4
IDX_DIM = 64
TOP_K_MAX = 256
M_HEADS = 4
M_QK = 64
M_V = 128
LRU_W = 512
LRU_BLOCKS = 8
CONV_W = 4
LRU_C = 8.0
N_BRANCH = 3
BRANCH_W = 512
EPS = 1e-6
NEG = -1e30

ATT_W = N_HEADS * HEAD_DIM
KV_W = N_KV_HEADS * HEAD_DIM
GROUP = N_HEADS // N_KV_HEADS

LANES = 128
ROW_PAD = 128
VMEM_LIMIT = 56 * 1024 * 1024
DSA_TILES_PER_GROUP = 1
MLSTM_SEQS = 2

SM_IK = 0
SM_IW = IDX_DIM
SM_MI = SM_IW + N_IDX_HEADS
SM_MF = SM_MI + M_HEADS
SM_W = 128


def _z_layout(d_model):
    widths = [("gt", N_BRANCH * d_model), ("aq", ATT_W), ("mv", M_HEADS * M_V), ("mo", M_HEADS * M_V),
              ("lx", LRU_W), ("lg", LRU_W), ("iq", N_IDX_HEADS * IDX_DIM), ("mq", M_HEADS * M_QK),
              ("mk", M_HEADS * M_QK), ("ak", KV_W), ("av", KV_W), ("sm", SM_W)]
    off, acc = {}, 0
    for name, w in widths:
        assert acc % w == 0, (name, acc, w)
        off[name] = (acc, w)
        acc += w
    return off, acc


def _pack_w_in(w_in, d_model, n_col_tiles):
    sizes = (ATT_W, KV_W, KV_W, N_IDX_HEADS * IDX_DIM, IDX_DIM, N_IDX_HEADS,
             M_HEADS * M_QK, M_HEADS * M_QK, M_HEADS * M_V, M_HEADS, M_HEADS, M_HEADS * M_V,
             LRU_W, LRU_W, N_BRANCH * d_model)
    names = ("aq", "ak", "av", "iq", "ik", "iw", "mq", "mk", "mv", "mi", "mf", "mo", "lx", "lg", "gt")
    assert w_in.shape[-1] == sum(sizes)
    parts, acc = {}, 0
    for n, s in zip(names, sizes):
        parts[n] = w_in[..., acc:acc + s]
        acc += s
    lead = w_in.shape[:-1]
    small = jnp.concatenate([parts["ik"], parts["iw"], parts["mi"], parts["mf"],
                             jnp.zeros(lead + (SM_W - SM_MF - M_HEADS,), w_in.dtype)], axis=-1)
    parts["sm"] = small
    off, zw = _z_layout(d_model)
    order = sorted(off, key=lambda n: off[n][0])
    cols = [parts[n] for n in order]
    zw_pad = -(-zw // (n_col_tiles * LANES)) * (n_col_tiles * LANES)
    if zw_pad > zw:
        cols.append(jnp.zeros(lead + (zw_pad - zw,), w_in.dtype))
    return jnp.concatenate(cols, axis=-1).astype(BF16), off, zw_pad


def _params(*sem):
    return pltpu.CompilerParams(dimension_semantics=sem, vmem_limit_bytes=VMEM_LIMIT)


def _bdot(a, b):
    return jnp.dot(a.astype(BF16), b.astype(BF16), preferred_element_type=F32)


def _bdot_nt(a, b):
    return lax.dot_general(a.astype(BF16), b.astype(BF16), (((1,), (1,)), ((), ())),
                           preferred_element_type=F32)


def _split3(a):
    a1 = a.astype(BF16)
    r1 = a - a1.astype(F32)
    a2 = r1.astype(BF16)
    a3 = (r1 - a2.astype(F32)).astype(BF16)
    return a1, a2, a3


def _dot3_nt(a, b):
    a1, a2, _ = _split3(a)
    b1, b2, _ = _split3(b)
    dn = (((1,), (1,)), ((), ()))
    f = lambda x, y: lax.dot_general(x, y, dn, preferred_element_type=F32)
    return f(a1, b1) + (f(a1, b2) + f(a2, b1))


def _dot3(a, b):
    a1, a2, _ = _split3(a)
    b1, b2, _ = _split3(b)
    f = lambda x, y: jnp.dot(x, y, preferred_element_type=F32)
    return f(a1, b1) + (f(a1, b2) + f(a2, b1))


def _rms_rows(x, g):
    return x * lax.rsqrt(jnp.mean(x * x, axis=-1, keepdims=True) + EPS) * g


def _sigmoid(x):
    return 1.0 / (1.0 + jnp.exp(-x))


def _log_sigmoid(x):
    return jnp.minimum(x, 0.0) - jnp.log1p(jnp.exp(-jnp.abs(x)))


def _softplus(x):
    return jnp.maximum(x, 0.0) + jnp.log1p(jnp.exp(-jnp.abs(x)))


def _gelu_tanh(x):
    c = np.float32(np.sqrt(2.0 / np.pi))
    return 0.5 * x * (1.0 + jnp.tanh(c * (x + np.float32(0.044715) * (x * x * x))))


def _inproj_kernel(x_ref, g_ref, w_ref, z_ref):
    hn = _rms_rows(x_ref[...], g_ref[...])
    z_ref[...] = jnp.dot(hn.astype(BF16), w_ref[...], preferred_element_type=F32)


def _inproj(x, g, w_packed, layer, *, tm, n_col_tiles):
    rows, d = x.shape
    zw = w_packed.shape[-1]
    tn = zw // n_col_tiles
    return pl.pallas_call(
        _inproj_kernel,
        grid=(n_col_tiles, rows // tm),
        in_specs=[pl.BlockSpec((tm, d), lambda j, i: (i, 0)),
                  pl.BlockSpec((None, 1, d), lambda j, i: (layer, 0, 0)),
                  pl.BlockSpec((None, d, tn), lambda j, i: (layer, 0, j))],
        out_specs=pl.BlockSpec((tm, tn), lambda j, i: (i, j)),
        out_shape=jax.ShapeDtypeStruct((rows, zw), F32),
        compiler_params=_params("parallel", "parallel"),
        name="inproj",
    )(x, g, w_packed)


def _kvprep_kernel(ak_ref, av_ref, sm_ref, kn_ref, ikn_ref, k_ref, v_ref, ki_ref, kb_ref, vt_ref, kic_ref,
                   *, t_valid):
    ak = ak_ref[...]
    kn = kn_ref[...]
    ks = [_rms_rows(ak[:, h * HEAD_DIM:(h + 1) * HEAD_DIM], kn) for h in range(N_KV_HEADS)]
    k_ref[...] = jnp.concatenate(ks, axis=-1)[:t_valid]
    av = av_ref[...]
    v_ref[...] = av[:t_valid]
    ki = _rms_rows(sm_ref[...][:, SM_IK:SM_IK + IDX_DIM], ikn_ref[...])
    ki_ref[...] = ki[:t_valid]
    for g in range(N_KV_HEADS):
        kb_ref[g] = ks[g].astype(BF16)
    vt_ref[...] = av.T.astype(BF16)
    hi = ki.astype(BF16).astype(F32)
    kic_ref[...] = jnp.concatenate([hi, ki - hi, hi, jnp.zeros_like(hi)], axis=-1).astype(BF16)


def _kvprep(z, off, k_norm, idx_k_norm, layer, *, n_batch, tp, t_valid):
    rows = z.shape[0]
    cb = lambda name: off[name][0] // off[name][1]
    return pl.pallas_call(
        functools.partial(_kvprep_kernel, t_valid=t_valid),
        grid=(n_batch,),
        in_specs=[pl.BlockSpec((tp, KV_W), lambda b: (b, cb("ak"))),
                  pl.BlockSpec((tp, KV_W), lambda b: (b, cb("av"))),
                  pl.BlockSpec((tp, SM_W), lambda b: (b, cb("sm"))),
                  pl.BlockSpec((None, 1, HEAD_DIM), lambda b: (layer, 0, 0)),
                  pl.BlockSpec((None, 1, IDX_DIM), lambda b: (layer, 0, 0))],
        out_specs=[pl.BlockSpec((None, t_valid, KV_W), lambda b: (b, 0, 0)),
                   pl.BlockSpec((None, t_valid, KV_W), lambda b: (b, 0, 0)),
                   pl.BlockSpec((None, t_valid, IDX_DIM), lambda b: (b, 0, 0)),
                   pl.BlockSpec((None, N_KV_HEADS, tp, HEAD_DIM), lambda b: (b, 0, 0, 0)),
                   pl.BlockSpec((None, KV_W, tp), lambda b: (b, 0, 0)),
                   pl.BlockSpec((None, tp, 4 * IDX_DIM), lambda b: (b, 0, 0))],
        out_shape=[jax.ShapeDtypeStruct((n_batch, t_valid, KV_W), F32),
                   jax.ShapeDtypeStruct((n_batch, t_valid, KV_W), F32),
                   jax.ShapeDtypeStruct((n_batch, t_valid, IDX_DIM), F32),
                   jax.ShapeDtypeStruct((n_batch, N_KV_HEADS, tp, HEAD_DIM), BF16),
                   jax.ShapeDtypeStruct((n_batch, KV_W, tp), BF16),
                   jax.ShapeDtypeStruct((n_batch, tp, 4 * IDX_DIM), BF16)],
        compiler_params=_params("parallel"),
        name="kvprep",
    )(z, z, z, k_norm, idx_k_norm)


_INT_MIN = np.int32(-2 ** 31)


def _sort_key_of(x):
    b = np.array(x, np.float32).view(np.int32)
    return np.int32(b ^ 0x7FFFFFFF) if b < 0 else np.int32(b)


_OK_KEY = _sort_key_of(0.5 * NEG)


def _dsa_kernel(aq_ref, iq_ref, sm_ref, k_ref, v_ref, ki_ref, qn_ref, o_ref, key_ref, msk_ref,
                *, tq, lp, l_valid, top_k, prompt):
    nblk = lp // LANES
    kf = np.float32(top_k)

    iq = iq_ref[...]
    sm = sm_ref[...]
    ki = ki_ref[...]
    sc = None
    for h in range(N_IDX_HEADS):
        s = _dot3_nt(iq[:, h * IDX_DIM:(h + 1) * IDX_DIM], ki)
        s = jnp.maximum(s * np.float32(IDX_DIM ** -0.5), 0.0)
        w = sm[:, SM_IW + h:SM_IW + h + 1] * np.float32(N_IDX_HEADS ** -0.5)
        sc = s * w if sc is None else sc + s * w
    sc = jnp.where(sc == 0.0, 0.0, sc)

    kpos = lax.broadcasted_iota(I32, (1, lp), 1)
    if prompt:
        qpos = pl.program_id(1) * tq + lax.broadcasted_iota(I32, (tq, 1), 0)
        allowed = ((kpos + (CHUNK - N_META)) // CHUNK <= (qpos + (CHUNK - N_META)) // CHUNK) & (kpos < l_valid)
    else:
        allowed = jnp.broadcast_to(kpos < l_valid, (tq, lp))
    scm = jnp.where(allowed, sc, np.float32(NEG))
    bits = pltpu.bitcast(scm, I32)
    key_ref[...] = jnp.where(bits < 0, bits ^ np.int32(0x7FFFFFFF), bits)

    def count_ge(cand):
        acc = jnp.zeros((tq, LANES), F32)
        for j in range(nblk):
            acc = acc + jnp.where(key_ref[:, j * LANES:(j + 1) * LANES] >= cand, 1.0, 0.0)
        return jnp.sum(acc, axis=-1, keepdims=True)

    ans = jnp.where(count_ge(jnp.zeros((tq, 1), I32)) >= kf, np.int32(0), _INT_MIN)

    def bit_step(t, ans):
        cand = ans | lax.shift_left(np.int32(1), np.int32(30) - t)
        return jnp.where(count_ge(cand) >= kf, cand, ans)

    thr = lax.fori_loop(0, 31, bit_step, ans)

    acc = jnp.zeros((tq, LANES), F32)
    for j in range(nblk):
        acc = acc + jnp.where(key_ref[:, j * LANES:(j + 1) * LANES] > thr, 1.0, 0.0)
    need = kf - jnp.sum(acc, axis=-1, keepdims=True)
    tri = (lax.broadcasted_iota(I32, (LANES, LANES), 0) < lax.broadcasted_iota(I32, (LANES, LANES), 1))
    tri = jnp.where(tri, 1.0, 0.0).astype(BF16)
    seen = jnp.zeros((tq, 1), F32)
    for j in range(nblk):
        kj = key_ref[:, j * LANES:(j + 1) * LANES]
        eq = jnp.where(kj == thr, 1.0, 0.0)
        before = seen + jnp.dot(eq.astype(BF16), tri, preferred_element_type=F32)
        sel = (kj > thr) | ((kj == thr) & (before < need))
        sel = sel & (kj > _OK_KEY)
        msk_ref[:, j * LANES:(j + 1) * LANES] = jnp.where(sel, 1.0, 0.0)
        seen = seen + jnp.sum(eq, axis=-1, keepdims=True)

    q = aq_ref[...]
    qn = qn_ref[...]
    kk = k_ref[...]
    vv = v_ref[...]
    sel = msk_ref[...] > 0.5
    outs = []
    for g in range(N_KV_HEADS):
        kg = kk[:, g * HEAD_DIM:(g + 1) * HEAD_DIM].astype(BF16)
        vg = vv[:, g * HEAD_DIM:(g + 1) * HEAD_DIM].astype(BF16)
        for hh in range(GROUP):
            h = g * GROUP + hh
            qh = _rms_rows(q[:, h * HEAD_DIM:(h + 1) * HEAD_DIM], qn)
            s = _bdot_nt(qh, kg) * np.float32(HEAD_DIM ** -0.5)
            s = jnp.where(sel, s, np.float32(NEG))
            p = jnp.exp(s - jnp.max(s, axis=-1, keepdims=True))
            den = jnp.sum(p, axis=-1, keepdims=True)
            outs.append(jnp.dot(p.astype(BF16), vg, preferred_element_type=F32) / den)
    o_ref[...] = jnp.concatenate(outs, axis=-1)


def _dsa(z, off, kmat, vmat, kimat, q_norm, layer, *, n_batch, rows_per_batch, tq, n_q, l_valid, top_k, prompt):
    lp = kmat.shape[1]
    assert lp % LANES == 0 and rows_per_batch % tq == 0
    rb = rows_per_batch // tq
    cb = lambda name: off[name][0] // off[name][1]
    kern = functools.partial(_dsa_kernel, tq=tq, lp=lp, l_valid=l_valid, top_k=top_k, prompt=prompt)
    return pl.pallas_call(
        kern,
        grid=(n_batch, n_q),
        in_specs=[pl.BlockSpec((tq, ATT_W), lambda b, i: (b * rb + i, cb("aq"))),
                  pl.BlockSpec((tq, N_IDX_HEADS * IDX_DIM), lambda b, i: (b * rb + i, cb("iq"))),
                  pl.BlockSpec((tq, SM_W), lambda b, i: (b * rb + i, cb("sm"))),
                  pl.BlockSpec((None, lp, KV_W), lambda b, i: (b, 0, 0)),
                  pl.BlockSpec((None, lp, KV_W), lambda b, i: (b, 0, 0)),
                  pl.BlockSpec((None, lp, IDX_DIM), lambda b, i: (b, 0, 0)),
                  pl.BlockSpec((None, 1, HEAD_DIM), lambda b, i: (layer, 0, 0))],
        out_specs=pl.BlockSpec((tq, ATT_W), lambda b, i: (b * n_q + i, 0)),
        out_shape=jax.ShapeDtypeStruct((n_batch * n_q * tq, ATT_W), F32),
        scratch_shapes=[pltpu.VMEM((tq, lp), I32), pltpu.VMEM((tq, lp), F32)],
        compiler_params=_params("parallel", "parallel"),
        name="dsa",
    )(z, z, z, kmat, vmat, kimat, q_norm)


def _row_blocks(x):
    return [x[j * LANES:(j + 1) * LANES, :] for j in range(x.shape[0] // LANES)]


def _col_reduce(blocks, op, red):
    a = blocks[0]
    for blk in blocks[1:]:
        a = op(a, blk)
    r = a.shape[0]
    while r > 8:
        r //= 2
        a = op(a[:r], a[r:])
    return red(a, axis=0, keepdims=True)


_WORD = 32
_SUB = 8


def _bit_planes(slabs):
    x = list(slabs)
    for d, m in ((16, 0x0000FFFF), (8, 0x00FF00FF), (4, 0x0F0F0F0F), (2, 0x33333333), (1, 0x55555555)):
        for i in range(_WORD):
            if not i & d:
                a, b = x[i], x[i + d]
                t = (lax.shift_right_logical(a, np.int32(d)) ^ b) & np.int32(m)
                x[i + d] = b ^ t
                x[i] = a ^ lax.shift_left(t, np.int32(d))
    return x


def _kth_largest(key_ref, lp, tq, top_k):
    n_pieces = lp // _SUB
    n_grp = -(-n_pieces // _WORD)
    slabs = []
    for i in range(_WORD):
        parts = []
        for g in range(n_grp):
            r0 = (g * _WORD + i) * _SUB
            parts.append(key_ref[r0:r0 + _SUB, :] ^ _INT_MIN if r0 < lp else jnp.zeros((_SUB, tq), I32))
        slabs.append(jnp.concatenate(parts, axis=0) if n_grp > 1 else parts[0])
    planes = _bit_planes(slabs)
    grp = lax.broadcasted_iota(I32, (n_grp * _SUB, tq), 0) // _SUB
    last = n_pieces - (n_grp - 1) * _WORD
    last_mask = np.int32(-1) if last == _WORD else np.int32((1 << last) - 1)
    act = jnp.where(grp < n_grp - 1, np.int32(-1), last_mask)
    above = jnp.zeros((1, tq), I32)
    thr_u = jnp.zeros((1, tq), I32)
    for b in range(_WORD - 1, -1, -1):
        ones = act & planes[b]
        n1 = jnp.sum(lax.population_count(ones), axis=0, keepdims=True)
        take = above + n1 >= top_k
        act = jnp.where(take, ones, act ^ ones)
        above = jnp.where(take, above, above + n1)
        thr_u = jnp.where(take, thr_u | (_INT_MIN if b == _WORD - 1 else np.int32(1 << b)), thr_u)
    n_eq = jnp.sum(lax.population_count(act), axis=0, keepdims=True)
    return thr_u ^ _INT_MIN, above + n_eq


def _order_key(bits):
    return jnp.where(bits < 0, _INT_MIN - bits, bits)


_OK_KEY_T = np.int32(-int(np.array(-0.5 * NEG, np.float32).view(np.int32)))


def _dsa_t_kernel(aq_ref, iq_ref, sm_ref, kb_ref, vt_ref, kic_ref, qn_ref, *rest, tq, lp, l_valid, top_k, q0):
    o_ref, key_ref, bias_ref = rest[-3:]
    kf = np.float32(top_k)
    nt = (((1,), (1,)), ((), ()))

    smt = sm_ref[...].T
    iq = iq_ref[...]
    pieces = []
    for h in range(N_IDX_HEADS):
        qh = iq[:, h * IDX_DIM:(h + 1) * IDX_DIM]
        hi = qh.astype(BF16).astype(F32)
        pieces.append(jnp.concatenate([hi, hi, qh - hi, jnp.zeros_like(hi)], axis=-1))
    qcat = jnp.concatenate(pieces, axis=0).astype(BF16)
    st = lax.dot_general(kic_ref[...], qcat, nt, preferred_element_type=F32)
    sc = None
    for h in range(N_IDX_HEADS):
        w = smt[SM_IW + h:SM_IW + h + 1, :] * np.float32(N_IDX_HEADS ** -0.5 * IDX_DIM ** -0.5)
        t = jnp.maximum(st[:, h * tq:(h + 1) * tq], 0.0) * w
        sc = t if sc is None else sc + t

    kpos = lax.broadcasted_iota(I32, (lp, 1), 0)
    qpos = (q0 + pl.program_id(1)) * tq + lax.broadcasted_iota(I32, (1, tq), 1)
    allowed = ((kpos + (CHUNK - N_META)) // CHUNK <= (qpos + (CHUNK - N_META)) // CHUNK) & (kpos < l_valid)
    scm = jnp.where(allowed, sc, np.float32(NEG))
    key_ref[...] = _order_key(pltpu.bitcast(scm, I32))

    thr, n_ge = _kth_largest(key_ref, lp, tq, top_k)
    exact = (jnp.min(jnp.where(n_ge == top_k, 1.0, 0.0)) > 0.5).astype(I32)

    @pl.when(exact == 1)
    def _():
        key = key_ref[...]
        bias_ref[...] = jnp.where((key >= thr) & (key > _OK_KEY_T), 0.0, np.float32(NEG))

    @pl.when(exact == 0)
    def _():
        need = kf - _col_reduce([jnp.where(key_ref[j * LANES:(j + 1) * LANES, :] > thr, 1.0, 0.0)
                                 for j in range(lp // LANES)], jnp.add, jnp.sum)
        tri = (lax.broadcasted_iota(I32, (LANES, LANES), 1) < lax.broadcasted_iota(I32, (LANES, LANES), 0))
        tri = jnp.where(tri, 1.0, 0.0).astype(BF16)
        seen = jnp.zeros((1, tq), F32)
        for j in range(lp // LANES):
            kj = key_ref[j * LANES:(j + 1) * LANES, :]
            eq = jnp.where(kj == thr, 1.0, 0.0)
            before = seen + jnp.dot(tri, eq.astype(BF16), preferred_element_type=F32)
            sel = ((kj > thr) | ((kj == thr) & (before < need))) & (kj > _OK_KEY_T)
            bias_ref[j * LANES:(j + 1) * LANES, :] = jnp.where(sel, 0.0, np.float32(NEG))
            seen = seen + jnp.sum(eq, axis=0, keepdims=True)

    q = aq_ref[...]
    qn = qn_ref[...] * np.float32(HEAD_DIM ** -0.5 * np.log2(np.e))
    bias = bias_ref[...]
    outs = []
    for g in range(N_KV_HEADS):
        qs = [_rms_rows(q[:, h * HEAD_DIM:(h + 1) * HEAD_DIM], qn) for h in range(g * GROUP, (g + 1) * GROUP)]
        qs = jnp.concatenate(qs, axis=0).astype(BF16)
        s_all = lax.dot_general(kb_ref[g], qs, nt, preferred_element_type=F32)
        vtg = vt_ref[g * HEAD_DIM:(g + 1) * HEAD_DIM, :]
        for hh in range(GROUP):
            s = s_all[:, hh * tq:(hh + 1) * tq] + bias
            p = jnp.exp2(s - _col_reduce(_row_blocks(s), jnp.maximum, jnp.max))
            den = _col_reduce(_row_blocks(p), jnp.add, jnp.sum)
            outs.append(jnp.dot(vtg, p.astype(BF16), preferred_element_type=F32) / den)
    o_ref[...] = jnp.concatenate(outs, axis=0).T


def _dsa_t(z, off, kb, vt, kic, q_norm, layer, out_buf, *, n_batch, tp, tq, q0, n_q, lp, l_valid, top_k):
    prev = [] if out_buf is None else [out_buf]
    rb = tp // tq
    cb = lambda name: off[name][0] // off[name][1]
    kern = functools.partial(_dsa_t_kernel, tq=tq, lp=lp, l_valid=l_valid, top_k=top_k, q0=q0)
    qrow = lambda b, i: b * rb + q0 + i
    return pl.pallas_call(
        kern,
        grid=(n_batch, n_q),
        in_specs=[pl.BlockSpec((tq, ATT_W), lambda b, i: (qrow(b, i), cb("aq"))),
                  pl.BlockSpec((tq, N_IDX_HEADS * IDX_DIM), lambda b, i: (qrow(b, i), cb("iq"))),
                  pl.BlockSpec((tq, SM_W), lambda b, i: (qrow(b, i), cb("sm"))),
                  pl.BlockSpec((None, N_KV_HEADS, lp, HEAD_DIM), lambda b, i: (b, 0, 0, 0)),
                  pl.BlockSpec((None, KV_W, lp), lambda b, i: (b, 0, 0)),
                  pl.BlockSpec((None, lp, 4 * IDX_DIM), lambda b, i: (b, 0, 0)),
                  pl.BlockSpec((None, 1, HEAD_DIM), lambda b, i: (layer, 0, 0))]
                 + [pl.BlockSpec(memory_space=pl.ANY) for _ in prev],
        out_specs=pl.BlockSpec((None, tq, ATT_W), lambda b, i: (b, q0 + i, 0)),
        out_shape=jax.ShapeDtypeStruct((n_batch, tp, ATT_W), F32),
        input_output_aliases={7: 0} if prev else {},
        scratch_shapes=[pltpu.VMEM((lp, tq), I32), pltpu.VMEM((lp, tq), F32)],
        compiler_params=_params("parallel", "parallel"),
        name="dsa_t",
    )(z, z, z, kb, vt, kic, q_norm, *prev)


def _mlstm_kernel(bi_ref, bf_ref, mq_ref, mk_ref, mv_ref, mo_ref, sm_ref, gn_ref, c0_ref, n0_ref, m0_ref,
                  yb_ref, c1_ref, n1_ref, m1_ref, c_s, n_s, m_s, *, lc, t_valid, layer, nbb):
    j = pl.program_id(1)

    @pl.when(j == 0)
    def _():
        c_s[...] = c0_ref[...]
        n_s[...] = n0_ref[...]
        m_s[...] = m0_ref[...]

    s_io = lax.broadcasted_iota(I32, (lc, lc), 0)
    t_io = lax.broadcasted_iota(I32, (lc, lc), 1)
    causal = s_io <= t_io
    tri = jnp.where(t_io <= s_io, 1.0, 0.0).astype(BF16)
    tri3 = jnp.concatenate([tri, tri, tri], axis=1)
    pos_col = j * lc + lax.broadcasted_iota(I32, (lc, 1), 0)
    pos_row = j * lc + lax.broadcasted_iota(I32, (1, lc), 1)
    lane = lax.broadcasted_iota(I32, (1, LANES), 1)
    nt = (((1,), (1,)), ((), ()))
    c_in = [[c_s[bb, h] for h in range(M_HEADS)] for bb in range(nbb)]
    n_in = [[n_s[bb, h] for h in range(M_HEADS)] for bb in range(nbb)]
    m_in = [m_s[bb] for bb in range(nbb)]
    c_out, n_out, m_outs, y_out = [], [], [], []
    heads = range(M_HEADS)
    lcat = lambda xs: jnp.concatenate(xs, axis=1)
    causal_h = lcat([causal] * M_HEADS)
    for bb in range(nbb):
        sm = sm_ref[bb]
        smt = sm.T
        mq = mq_ref[bb]
        mk = mk_ref[bb]
        mv = mv_ref[bb]
        mo = mo_ref[bb]
        m_all = m_in[bb]
        i_col = [jnp.where(pos_col < t_valid, sm[:, SM_MI + h:SM_MI + h + 1] + bi_ref[layer, h], np.float32(NEG))
                 for h in heads]
        f_col = [jnp.where(pos_col < t_valid, _log_sigmoid(sm[:, SM_MF + h:SM_MF + h + 1] + bf_ref[layer, h]), 0.0)
                 for h in heads]
        i_row = lcat([jnp.where(pos_row < t_valid, smt[SM_MI + h:SM_MI + h + 1, :] + bi_ref[layer, h],
                                np.float32(NEG)) for h in heads])
        f_b = lcat([jnp.broadcast_to(f_col[h], (lc, lc)) for h in heads])
        i_b = lcat([jnp.broadcast_to(i_col[h], (lc, lc)) for h in heads])
        b_row = jnp.sum(jnp.where(causal_h, f_b, 0.0), axis=0, keepdims=True)
        p1, p2, p3 = _split3(f_b)
        b_src = jnp.dot(tri3, jnp.concatenate([p1, p2, p3], axis=0), preferred_element_type=F32)
        m_prev = lcat([jnp.broadcast_to(m_all[:, h:h + 1], (1, lc)) for h in heads])
        log_d = jnp.where(causal_h, b_row - b_src + i_b, np.float32(NEG))
        m_t = jnp.maximum(b_row + m_prev, jnp.max(log_d, axis=0, keepdims=True))
        d = jnp.exp(log_d - m_t)
        inter = jnp.exp(b_row + m_prev - m_t)
        qh = [mq[:, h * M_QK:(h + 1) * M_QK].astype(BF16) for h in heads]
        kh = [(mk[:, h * M_QK:(h + 1) * M_QK] * np.float32(M_QK ** -0.5)).astype(BF16) for h in heads]
        vt = [mv[:, h * M_V:(h + 1) * M_V].T for h in heads]
        w = lcat([lax.dot_general(kh[h], qh[h], nt, preferred_element_type=F32) for h in heads]) * d
        cq = lcat([lax.dot_general(
            jnp.concatenate([c_in[bb][h], jnp.broadcast_to(n_in[bb][h], (_SUB, M_QK))], axis=0).astype(BF16),
            qh[h], nt, preferred_element_type=F32) for h in heads])
        wb = w.astype(BF16)
        num = lcat([jnp.dot(vt[h].astype(BF16), wb[:, h * lc:(h + 1) * lc], preferred_element_type=F32)
                    for h in heads]) + inter * cq[:M_V]
        den = jnp.sum(w, axis=0, keepdims=True) + inter * cq[M_V:M_V + 1]
        hm = num / jnp.maximum(jnp.abs(den), jnp.exp(-m_t))
        last = lambda r: lcat([jnp.broadcast_to(r[:, (h + 1) * lc - 1:(h + 1) * lc], (1, lc)) for h in heads])
        m_new = last(m_t)
        b_last = last(b_row)
        decay = jnp.exp(b_last + m_prev - m_new)
        ws = jnp.exp(b_last - b_row + i_row - m_new)
        ws_hi = ws.astype(BF16).astype(F32)
        ws_lo = ws - ws_hi
        y = hm * lax.rsqrt(jnp.mean(hm * hm, axis=0, keepdims=True) + EPS) * gn_ref[...]
        m_out = m_all
        for h in heads:
            blk = slice(h * lc, (h + 1) * lc)
            upd = jnp.concatenate([vt[h] * ws[:, blk], ws_hi[:, blk], ws_lo[:, blk],
                                   jnp.zeros((_SUB - 2, lc), F32)], axis=0)
            upd = jnp.dot(upd.astype(BF16), kh[h], preferred_element_type=F32)
            dec = decay[:, h * lc:h * lc + 1]
            c_out.append(dec * c_in[bb][h] + upd[:M_V])
            n_out.append(dec * n_in[bb][h] + upd[M_V:M_V + 1] + upd[M_V + 1:M_V + 2])
            m_out = jnp.where(lane == h, m_new[:, h * lc:h * lc + 1], m_out)
            y_out.append(y[:, blk].T * _sigmoid(mo[:, h * M_V:(h + 1) * M_V]))
        m_outs.append(m_out)
    for bb in range(nbb):
        for h in range(M_HEADS):
            c_s[bb, h] = c_out[bb * M_HEADS + h]
            n_s[bb, h] = n_out[bb * M_HEADS + h]
            yb_ref[bb, :, h * M_V:(h + 1) * M_V] = y_out[bb * M_HEADS + h]
        m_s[bb] = m_outs[bb]

    @pl.when(j == pl.num_programs(1) - 1)
    def _():
        c1_ref[...] = c_s[...]
        n1_ref[...] = n_s[...]
        m1_ref[...] = m_s[...]


def _mlstm(z, off, bias_i, bias_f, m_norm, c0, n0, m0, layer, *, n_batch, tp, t_valid, lc, nbb):
    assert n_batch % nbb == 0 and lc == LANES
    nc = tp // lc
    z3 = z.reshape(n_batch, tp, z.shape[-1])
    cb = lambda name: off[name][0] // off[name][1]
    row = lambda name: pl.BlockSpec((nbb, lc, off[name][1]), lambda b, j: (b, j, cb(name)))
    smem = pl.BlockSpec(memory_space=pltpu.SMEM)
    st_c = pl.BlockSpec((nbb, M_HEADS, M_V, M_QK), lambda b, j: (b, 0, 0, 0))
    st_n = pl.BlockSpec((nbb, M_HEADS, 1, M_QK), lambda b, j: (b, 0, 0, 0))
    st_m = pl.BlockSpec((nbb, 1, LANES), lambda b, j: (b, 0, 0))
    kern = functools.partial(_mlstm_kernel, lc=lc, t_valid=t_valid, layer=layer, nbb=nbb)
    yb, c1, n1, m1 = pl.pallas_call(
        kern,
        grid=(n_batch // nbb, nc),
        in_specs=[smem, smem, row("mq"), row("mk"), row("mv"), row("mo"), row("sm"),
                  pl.BlockSpec((None, M_V, M_HEADS * LANES), lambda b, j: (layer, 0, 0)),
                  st_c, st_n, st_m],
        out_specs=[pl.BlockSpec((nbb, lc, M_HEADS * M_V), lambda b, j: (b, j, 0)), st_c, st_n, st_m],
        out_shape=[jax.ShapeDtypeStruct((n_batch, tp, M_HEADS * M_V), F32),
                   jax.ShapeDtypeStruct(c0.shape, F32), jax.ShapeDtypeStruct(n0.shape, F32),
                   jax.ShapeDtypeStruct(m0.shape, F32)],
        scratch_shapes=[pltpu.VMEM((nbb, M_HEADS, M_V, M_QK), F32), pltpu.VMEM((nbb, M_HEADS, 1, M_QK), F32),
                        pltpu.VMEM((nbb, 1, LANES), F32)],
        compiler_params=_params("parallel", "arbitrary"),
        name="mlstm",
    )(bias_i, bias_f, z3, z3, z3, z3, z3, m_norm, c0, n0, m0)
    return yb.reshape(n_batch * tp, M_HEADS * M_V), c1, n1, m1


def _shift_rows_in_group(x, s, fill):
    rolled = pltpu.roll(x, s, 0)
    row = lax.broadcasted_iota(I32, x.shape, 0)
    return jnp.where((row & 7) >= s, rolled, fill)


def _lru_kernel(lx_ref, lg_ref, cw_ref, cb_ref, w3_ref, ba_ref, bx_ref, lam_ref, h0_ref, buf0_ref,
                yc_ref, h1_ref, buf1_ref, prev_s, h_s, *, tt, t_valid):
    j = pl.program_id(1)

    @pl.when(j == 0)
    def _():
        prev_s[...] = buf0_ref[...]
        h_s[...] = h0_ref[...]

    lx = lx_ref[...]
    cw = cw_ref[...]
    prev = prev_s[...]
    row8 = lax.broadcasted_iota(I32, prev.shape, 0)
    xc = cb_ref[...] + lx * cw[CONV_W - 1:CONV_W, :]
    for s in range(1, CONV_W):
        rolled = pltpu.roll(lx, s, 0)
        head = jnp.where(row8 < s, pltpu.roll(prev, s, 0), rolled[:8])
        shifted = jnp.concatenate([head, rolled[8:]], axis=0) if tt > 8 else head
        xc = xc + shifted * cw[CONV_W - 1 - s:CONV_W - s, :]
    prev_s[...] = lx[tt - 8:, :]

    x_hi = xc.astype(BF16)
    x_lo = (xc - x_hi.astype(F32)).astype(BF16)
    gates = jnp.dot(jnp.concatenate([x_hi, x_hi, x_lo], axis=1), w3_ref[...], preferred_element_type=F32)
    r = _sigmoid(gates[:, :LRU_W] + ba_ref[...])
    gi = _sigmoid(gates[:, LRU_W:] + bx_ref[...])
    log_a = np.float32(-LRU_C) * r * _softplus(-lam_ref[...])
    a = jnp.exp(log_a)
    u = jnp.sqrt(jnp.tanh(-log_a) * (a * a + 1.0)) * gi * xc
    s = 1
    while s < 8:
        u = a * _shift_rows_in_group(u, s, 0.0) + u
        a = a * _shift_rows_in_group(a, s, 1.0)
        s *= 2
    carry = h_s[...]
    hs = []
    for g in range(tt // 8):
        hg = a[g * 8:(g + 1) * 8, :] * carry + u[g * 8:(g + 1) * 8, :]
        carry = hg[7:8, :]
        hs.append(hg)
    h = jnp.concatenate(hs, axis=0)
    h_s[...] = carry
    yc_ref[...] = h * _gelu_tanh(lg_ref[...])

    last = t_valid - 1

    @pl.when(j == last // tt)
    def _():
        r0 = last % tt
        h1_ref[...] = h[r0:r0 + 1, :]
        buf1_ref[...] = lx[r0 - 7:r0 + 1, :]


def _lru(z, off, conv_w, conv_b, w3, ba, bx, lam, h0, buf0, layer, *, n_batch, tp, t_valid, tt):
    assert t_valid % 8 == 0 and tt % 8 == 0
    nt = tp // tt
    cb = lambda name: off[name][0] // off[name][1]
    row = lambda name: pl.BlockSpec((tt, LRU_W), lambda b, j: (b * nt + j, cb(name)))
    vec = lambda n: pl.BlockSpec((None, n, LRU_W), lambda b, j: (layer, 0, 0))
    st_h = pl.BlockSpec((None, 1, LRU_W), lambda b, j: (b, 0, 0))
    st_b = pl.BlockSpec((None, 8, LRU_W), lambda b, j: (b, 0, 0))
    kern = functools.partial(_lru_kernel, tt=tt, t_valid=t_valid)
    return pl.pallas_call(
        kern,
        grid=(n_batch, nt),
        in_specs=[row("lx"), row("lg"), vec(CONV_W), vec(1),
                  pl.BlockSpec((None, 3 * LRU_W, 2 * LRU_W), lambda b, j: (layer, 0, 0)), vec(1), vec(1), vec(1),
                  st_h, st_b],
        out_specs=[pl.BlockSpec((tt, LRU_W), lambda b, j: (b * nt + j, 0)), st_h, st_b],
        out_shape=[jax.ShapeDtypeStruct((n_batch * tp, LRU_W), F32),
                   jax.ShapeDtypeStruct(h0.shape, F32), jax.ShapeDtypeStruct(buf0.shape, F32)],
        scratch_shapes=[pltpu.VMEM((8, LRU_W), F32), pltpu.VMEM((1, LRU_W), F32)],
        compiler_params=_params("parallel", "arbitrary"),
        name="lru",
    )(z, z, conv_w, conv_b, w3, ba, bx, lam, h0, buf0)


def _merge_kernel(x_ref, ya_ref, yb_ref, yc_ref, gt_ref, wb_ref, wo_ref, o_ref, *, d):
    acc = None
    for b, y_ref in enumerate((ya_ref, yb_ref, yc_ref)):
        proj = jnp.dot(y_ref[...].astype(BF16), wb_ref[b], preferred_element_type=F32)
        t = _sigmoid(gt_ref[:, b * d:(b + 1) * d]) * proj
        acc = t if acc is None else acc + t
    o_ref[...] = x_ref[...] + jnp.dot(acc.astype(BF16), wo_ref[...], preferred_element_type=F32)


def _merge(x, ya, yb, yc, z, off, w_branch, w_out, layer, *, tm):
    rows, d = x.shape
    assert off["gt"][0] == 0
    rowspec = lambda w: pl.BlockSpec((tm, w), lambda i: (i, 0))
    return pl.pallas_call(
        functools.partial(_merge_kernel, d=d),
        grid=(rows // tm,),
        in_specs=[rowspec(d), rowspec(BRANCH_W), rowspec(BRANCH_W), rowspec(BRANCH_W), rowspec(N_BRANCH * d),
                  pl.BlockSpec((None, N_BRANCH, BRANCH_W, d), lambda i: (layer, 0, 0, 0)),
                  pl.BlockSpec((None, d, d), lambda i: (layer, 0, 0))],
        out_specs=rowspec(d),
        out_shape=jax.ShapeDtypeStruct((rows, d), F32),
        compiler_params=_params("parallel"),
        name="merge",
    )(x, ya, yb, yc, z, w_branch, w_out)


def _ffn_kernel(x_ref, g_ref, wu_ref, wd_ref, o_ref, hf_s, acc_s):
    j = pl.program_id(1)

    @pl.when(j == 0)
    def _():
        hf_s[...] = _rms_rows(x_ref[...], g_ref[...]).astype(BF16)
        acc_s[...] = jnp.zeros_like(acc_s)

    up = jnp.dot(hf_s[...], wu_ref[...], preferred_element_type=F32)
    act = jnp.square(jnp.maximum(up, 0.0))
    acc_s[...] += jnp.dot(act.astype(BF16), wd_ref[...], preferred_element_type=F32)

    @pl.when(j == pl.num_programs(1) - 1)
    def _():
        o_ref[...] = x_ref[...] + acc_s[...]


def _ffn(x, g, w_up, w_down, layer, *, tm, tf):
    rows, d = x.shape
    dff = w_up.shape[-1]
    return pl.pallas_call(
        _ffn_kernel,
        grid=(rows // tm, dff // tf),
        in_specs=[pl.BlockSpec((tm, d), lambda i, j: (i, 0)),
                  pl.BlockSpec((None, 1, d), lambda i, j: (layer, 0, 0)),
                  pl.BlockSpec((None, d, tf), lambda i, j: (layer, 0, j)),
                  pl.BlockSpec((None, tf, d), lambda i, j: (layer, j, 0))],
        out_specs=pl.BlockSpec((tm, d), lambda i, j: (i, 0)),
        out_shape=jax.ShapeDtypeStruct((rows, d), F32),
        scratch_shapes=[pltpu.VMEM((tm, d), BF16), pltpu.VMEM((tm, d), F32)],
        compiler_params=_params("parallel", "arbitrary"),
        name="ffn",
    )(x, g, w_up, w_down)


def _final_kernel(x_ref, g_ref, o_ref, *, r0, n_out):
    o_ref[...] = _rms_rows(x_ref[r0:r0 + n_out, :], g_ref[...])


def _final_norm(x3, g, *, r0, n_out):
    nb, tp, d = x3.shape
    return pl.pallas_call(
        functools.partial(_final_kernel, r0=r0, n_out=n_out),
        grid=(nb,),
        in_specs=[pl.BlockSpec((None, tp, d), lambda b: (b, 0, 0)), pl.BlockSpec((1, d), lambda b: (0, 0))],
        out_specs=pl.BlockSpec((None, n_out, d), lambda b: (b, 0, 0)),
        out_shape=jax.ShapeDtypeStruct((nb, n_out, d), F32),
        compiler_params=_params("parallel"),
        name="final_norm",
    )(x3, g)


def _pick_tile(n, pref):
    t = pref
    while n % t:
        t //= 2
    return t


def _layer(x, p, layer, st, *, n_batch, tp, t_valid, prompt):
    d = x.shape[1]
    rows = n_batch * tp
    off = p["off"]
    tm = _pick_tile(rows, 512)
    z = _inproj(x, p["norm_mix"], p["w_in"], layer, tm=tm, n_col_tiles=p["n_col_tiles"])
    k_new, v_new, ki_new, kb, vt, kic = _kvprep(z, off, p["k_norm"], p["idx_k_norm"], layer, n_batch=n_batch,
                                                tp=tp, t_valid=t_valid)

    if prompt:
        top_k = min(TOP_K_MAX, (t_valid - N_META) // 4)
        tq, n_q = ROW_PAD, tp // ROW_PAD
        ya3, i0 = None, 0
        while i0 < n_q:
            i1 = n_q if n_q - i0 <= DSA_TILES_PER_GROUP + 1 else i0 + DSA_TILES_PER_GROUP
            lp = min(tp, -(-(i1 * tq + CHUNK) // LANES) * LANES)
            assert lp >= top_k
            ya3 = _dsa_t(z, off, kb, vt, kic, p["q_norm"], layer, ya3, n_batch=n_batch, tp=tp, tq=tq, q0=i0,
                         n_q=i1 - i0, lp=lp, l_valid=t_valid, top_k=top_k)
            i0 = i1
        ya = ya3.reshape(rows, ATT_W)
    else:
        past = st[0].shape[1]
        l_valid = past + t_valid
        lp = -(-l_valid // LANES) * LANES
        cat = lambda old, new: jnp.concatenate(
            [old.reshape(n_batch, past, -1).astype(F32), new,
             jnp.zeros((n_batch, lp - l_valid, new.shape[-1]), F32)], axis=1)
        kmat, vmat, kimat = cat(st[0], k_new), cat(st[1], v_new), cat(st[2], ki_new)
        top_k = min(TOP_K_MAX, l_valid // 4)
        tq = t_valid
        ya = _dsa(z, off, kmat, vmat, kimat, p["q_norm"], layer, n_batch=n_batch, rows_per_batch=tp, tq=tq,
                  n_q=1, l_valid=l_valid, top_k=top_k, prompt=False)
        ya = jnp.pad(ya.reshape(n_batch, tq, ATT_W), ((0, 0), (0, tp - tq), (0, 0))).reshape(rows, ATT_W)

    if prompt:
        c0 = jnp.zeros((n_batch, M_HEADS, M_V, M_QK), F32)
        n0 = jnp.zeros((n_batch, M_HEADS, 1, M_QK), F32)
        m0 = jnp.zeros((n_batch, 1, LANES), F32)
        h0 = jnp.zeros((n_batch, 1, LRU_W), F32)
        buf0 = jnp.zeros((n_batch, 8, LRU_W), F32)
    else:
        c0 = st[3].astype(F32)
        n0 = st[4].astype(F32).reshape(n_batch, M_HEADS, 1, M_QK)
        m0 = jnp.pad(st[5].astype(F32), ((0, 0), (0, LANES - M_HEADS))).reshape(n_batch, 1, LANES)
        h0 = st[6].astype(F32).reshape(n_batch, 1, LRU_W)
        buf0 = jnp.pad(st[7].astype(F32), ((0, 0), (8 - (CONV_W - 1), 0), (0, 0)))
    yb, c1, n1, m1 = _mlstm(z, off, p["m_bias_i"], p["m_bias_f"], p["m_norm"], c0, n0, m0, layer,
                            n_batch=n_batch, tp=tp, t_valid=t_valid, lc=ROW_PAD, nbb=_pick_tile(n_batch, MLSTM_SEQS))
    yc, h1, buf1 = _lru(z, off, p["conv_w"], p["conv_b"], p["lru_w3"], p["lru_ba"], p["lru_bx"],
                        p["lru_lambda"], h0, buf0, layer, n_batch=n_batch, tp=tp, t_valid=t_valid, tt=ROW_PAD)

    x = _merge(x, ya, yb, yc, z, off, p["w_branch"], p["w_out"], layer, tm=tm)
    x = _ffn(x, p["norm_ffn"], p["w_up"], p["w_down"], layer, tm=tm, tf=_pick_tile(p["w_up"].shape[-1], 2048))
    state = (k_new.reshape(n_batch, t_valid, N_KV_HEADS, HEAD_DIM),
             v_new.reshape(n_batch, t_valid, N_KV_HEADS, HEAD_DIM), ki_new,
             c1, n1.reshape(n_batch, M_HEADS, M_QK), m1[:, 0, :M_HEADS],
             h1.reshape(n_batch, LRU_W), buf1[:, 8 - (CONV_W - 1):])
    return x, state


def _block_diag(w):
    depth, nb, c, _ = w.shape
    eye = jnp.eye(nb, dtype=w.dtype)
    return jnp.einsum("lncd,nm->lncmd", w, eye).reshape(depth, nb * c, nb * c)


def _gate_weights(wa, wx):
    w = jnp.concatenate([_block_diag(wa.astype(F32)), _block_diag(wx.astype(F32))], axis=-1)
    hi = w.astype(BF16)
    lo = (w - hi.astype(F32)).astype(BF16)
    return jnp.concatenate([hi, lo, hi], axis=1)


def kernel(x_prompt, x_sample, cache_k, cache_v, cache_idx_k, state_mlstm_c, state_mlstm_n, state_mlstm_m, state_lru_h, state_conv, meta_tokens, norm_mix, w_in, q_norm, k_norm, idx_k_norm, m_bias_i, m_bias_f, m_norm, conv_w, conv_b, lru_wa, lru_ba, lru_wx, lru_bx, lru_lambda, w_branch, w_out, norm_ffn, w_up, w_down, final_norm):
    depth = w_in.shape[0]
    nb, seq, d = x_prompt.shape
    nbs, dseq, _ = x_sample.shape
    n_col_tiles = 3
    w_packed, off, _ = _pack_w_in(w_in, d, n_col_tiles)
    vec3 = lambda a: a.astype(F32).reshape(depth, 1, -1)
    p = dict(off=off, n_col_tiles=n_col_tiles, w_in=w_packed,
             norm_mix=vec3(norm_mix), q_norm=vec3(q_norm), k_norm=vec3(k_norm), idx_k_norm=vec3(idx_k_norm),
             m_bias_i=m_bias_i.astype(F32), m_bias_f=m_bias_f.astype(F32), m_norm=jnp.repeat(jnp.swapaxes(m_norm.astype(F32), 1, 2), LANES, axis=2),
             conv_w=conv_w.astype(F32), conv_b=vec3(conv_b),
             lru_w3=_gate_weights(lru_wa, lru_wx), lru_ba=vec3(lru_ba), lru_bx=vec3(lru_bx),
             lru_lambda=vec3(lru_lambda),
             w_branch=w_branch.astype(BF16), w_out=w_out.astype(BF16), norm_ffn=vec3(norm_ffn),
             w_up=w_up.astype(BF16), w_down=w_down.astype(BF16))

    t_p = N_META + seq
    tp_p = -(-t_p // ROW_PAD) * ROW_PAD
    meta = jnp.broadcast_to(meta_tokens[None].astype(x_prompt.dtype), (nb, N_META, d))
    hp = jnp.concatenate([meta, x_prompt, jnp.zeros((nb, tp_p - t_p, d), x_prompt.dtype)], axis=1)
    hp = hp.reshape(nb * tp_p, d)
    tp_s = -(-dseq // ROW_PAD) * ROW_PAD
    hs = jnp.pad(x_sample, ((0, 0), (0, tp_s - dseq), (0, 0))).reshape(nbs * tp_s, d)

    p_states, s_states = [], []
    for l in range(depth):
        hp, stp = _layer(hp, p, l, None, n_batch=nb, tp=tp_p, t_valid=t_p, prompt=True)
        hs, sts = _layer(hs, p, l, (cache_k[l], cache_v[l], cache_idx_k[l], state_mlstm_c[l], state_mlstm_n[l],
                                    state_mlstm_m[l], state_lru_h[l], state_conv[l]),
                         n_batch=nbs, tp=tp_s, t_valid=dseq, prompt=False)
        p_states.append(stp)
        s_states.append(sts)
    fg = final_norm.astype(F32).reshape(1, d)
    y_prompt = _final_norm(hp.reshape(nb, tp_p, d), fg, r0=N_META, n_out=seq)
    y_sample = _final_norm(hs.reshape(nbs, tp_s, d), fg, r0=0, n_out=dseq)
    pouts = [jnp.stack(t) for t in zip(*p_states)]
    souts = [jnp.stack(t) for t in zip(*s_states)]
    return (y_prompt, y_sample, *pouts, *souts)
```

```python
import functools

import numpy as np
import jax
import jax.numpy as jnp
from jax import lax
from jax.experimental import pallas as pl
from jax.experimental.pallas import tpu as pltpu

F32 = jnp.float32
BF16 = jnp.bfloat16
I32 = jnp.int32

CHUNK = 64
N_META = 16
N_HEADS = 8
N_KV_HEADS = 2
HEAD_DIM = 64
N_IDX_HEADS = 4
IDX_DIM = 64
TOP_K_MAX = 256
M_HEADS = 4
M_QK = 64
M_V = 128
LRU_W = 512
LRU_BLOCKS = 8
CONV_W = 4
LRU_C = 8.0
N_BRANCH = 3
BRANCH_W = 512
EPS = 1e-6
NEG = -1e30

ATT_W = N_HEADS * HEAD_DIM
KV_W = N_KV_HEADS * HEAD_DIM
GROUP = N_HEADS // N_KV_HEADS

LANES = 128
ROW_PAD = 128
VMEM_LIMIT = 56 * 1024 * 1024
DSA_TILES_PER_GROUP = 1
MLSTM_SEQS = 2

SM_IK = 0
SM_IW = IDX_DIM
SM_MI = SM_IW + N_IDX_HEADS
SM_MF = SM_MI + M_HEADS
SM_W = 128


def _z_layout(d_model):
    widths = [("gt", N_BRANCH * d_model), ("aq", ATT_W), ("mv", M_HEADS * M_V), ("mo", M_HEADS * M_V),
              ("lx", LRU_W), ("lg", LRU_W), ("iq", N_IDX_HEADS * IDX_DIM), ("mq", M_HEADS * M_QK),
              ("mk", M_HEADS * M_QK), ("ak", KV_W), ("av", KV_W), ("sm", SM_W)]
    off, acc = {}, 0
    for name, w in widths:
        assert acc % w == 0, (name, acc, w)
        off[name] = (acc, w)
        acc += w
    return off, acc


def _pack_w_in(w_in, d_model, n_col_tiles):
    sizes = (ATT_W, KV_W, KV_W, N_IDX_HEADS * IDX_DIM, IDX_DIM, N_IDX_HEADS,
             M_HEADS * M_QK, M_HEADS * M_QK, M_HEADS * M_V, M_HEADS, M_HEADS, M_HEADS * M_V,
             LRU_W, LRU_W, N_BRANCH * d_model)
    names = ("aq", "ak", "av", "iq", "ik", "iw", "mq", "mk", "mv", "mi", "mf", "mo", "lx", "lg", "gt")
    assert w_in.shape[-1] == sum(sizes)
    parts, acc = {}, 0
    for n, s in zip(names, sizes):
        parts[n] = w_in[..., acc:acc + s]
        acc += s
    lead = w_in.shape[:-1]
    small = jnp.concatenate([parts["ik"], parts["iw"], parts["mi"], parts["mf"],
                             jnp.zeros(lead + (SM_W - SM_MF - M_HEADS,), w_in.dtype)], axis=-1)
    parts["sm"] = small
    off, zw = _z_layout(d_model)
    order = sorted(off, key=lambda n: off[n][0])
    cols = [parts[n] for n in order]
    zw_pad = -(-zw // (n_col_tiles * LANES)) * (n_col_tiles * LANES)
    if zw_pad > zw:
        cols.append(jnp.zeros(lead + (zw_pad - zw,), w_in.dtype))
    return jnp.concatenate(cols, axis=-1).astype(BF16), off, zw_pad


def _params(*sem):
    return pltpu.CompilerParams(dimension_semantics=sem, vmem_limit_bytes=VMEM_LIMIT)


def _bdot(a, b):
    return jnp.dot(a.astype(BF16), b.astype(BF16), preferred_element_type=F32)


def _bdot_nt(a, b):
    return lax.dot_general(a.astype(BF16), b.astype(BF16), (((1,), (1,)), ((), ())),
                           preferred_element_type=F32)


def _split3(a):
    a1 = a.astype(BF16)
    r1 = a - a1.astype(F32)
    a2 = r1.astype(BF16)
    a3 = (r1 - a2.astype(F32)).astype(BF16)
    return a1, a2, a3


def _dot3_nt(a, b):
    a1, a2, _ = _split3(a)
    b1, b2, _ = _split3(b)
    dn = (((1,), (1,)), ((), ()))
    f = lambda x, y: lax.dot_general(x, y, dn, preferred_element_type=F32)
    return f(a1, b1) + (f(a1, b2) + f(a2, b1))


def _dot3(a, b):
    a1, a2, _ = _split3(a)
    b1, b2, _ = _split3(b)
    f = lambda x, y: jnp.dot(x, y, preferred_element_type=F32)
    return f(a1, b1) + (f(a1, b2) + f(a2, b1))


def _rms_rows(x, g):
    return x * lax.rsqrt(jnp.mean(x * x, axis=-1, keepdims=True) + EPS) * g


def _sigmoid(x):
    return 1.0 / (1.0 + jnp.exp(-x))


def _log_sigmoid(x):
    return jnp.minimum(x, 0.0) - jnp.log1p(jnp.exp(-jnp.abs(x)))


def _softplus(x):
    return jnp.maximum(x, 0.0) + jnp.log1p(jnp.exp(-jnp.abs(x)))


def _gelu_tanh(x):
    c = np.float32(np.sqrt(2.0 / np.pi))
    return 0.5 * x * (1.0 + jnp.tanh(c * (x + np.float32(0.044715) * (x * x * x))))


def _inproj_kernel(x_ref, g_ref, w_ref, z_ref):
    hn = _rms_rows(x_ref[...], g_ref[...])
    z_ref[...] = jnp.dot(hn.astype(BF16), w_ref[...], preferred_element_type=F32)


def _inproj(x, g, w_packed, layer, *, tm, n_col_tiles):
    rows, d = x.shape
    zw = w_packed.shape[-1]
    tn = zw // n_col_tiles
    return pl.pallas_call(
        _inproj_kernel,
        grid=(n_col_tiles, rows // tm),
        in_specs=[pl.BlockSpec((tm, d), lambda j, i: (i, 0)),
                  pl.BlockSpec((None, 1, d), lambda j, i: (layer, 0, 0)),
                  pl.BlockSpec((None, d, tn), lambda j, i: (layer, 0, j))],
        out_specs=pl.BlockSpec((tm, tn), lambda j, i: (i, j)),
        out_shape=jax.ShapeDtypeStruct((rows, zw), F32),
        compiler_params=_params("parallel", "parallel"),
        name="inproj",
    )(x, g, w_packed)


def _kvprep_kernel(ak_ref, av_ref, sm_ref, kn_ref, ikn_ref, k_ref, v_ref, ki_ref, kb_ref, vt_ref, kic_ref,
                   *, t_valid):
    ak = ak_ref[...]
    kn = kn_ref[...]
    ks = [_rms_rows(ak[:, h * HEAD_DIM:(h + 1) * HEAD_DIM], kn) for h in range(N_KV_HEADS)]
    k_ref[...] = jnp.concatenate(ks, axis=-1)[:t_valid]
    av = av_ref[...]
    v_ref[...] = av[:t_valid]
    ki = _rms_rows(sm_ref[...][:, SM_IK:SM_IK + IDX_DIM], ikn_ref[...])
    ki_ref[...] = ki[:t_valid]
    for g in range(N_KV_HEADS):
        kb_ref[g] = ks[g].astype(BF16)
    vt_ref[...] = av.T.astype(BF16)
    hi = ki.astype(BF16).astype(F32)
    kic_ref[...] = jnp.concatenate([hi, ki - hi, hi, jnp.zeros_like(hi)], axis=-1).astype(BF16)


def _kvprep(z, off, k_norm, idx_k_norm, layer, *, n_batch, tp, t_valid):
    rows = z.shape[0]
    cb = lambda name: off[name][0] // off[name][1]
    return pl.pallas_call(
        functools.partial(_kvprep_kernel, t_valid=t_valid),
        grid=(n_batch,),
        in_specs=[pl.BlockSpec((tp, KV_W), lambda b: (b, cb("ak"))),
                  pl.BlockSpec((tp, KV_W), lambda b: (b, cb("av"))),
                  pl.BlockSpec((tp, SM_W), lambda b: (b, cb("sm"))),
                  pl.BlockSpec((None, 1, HEAD_DIM), lambda b: (layer, 0, 0)),
                  pl.BlockSpec((None, 1, IDX_DIM), lambda b: (layer, 0, 0))],
        out_specs=[pl.BlockSpec((None, t_valid, KV_W), lambda b: (b, 0, 0)),
                   pl.BlockSpec((None, t_valid, KV_W), lambda b: (b, 0, 0)),
                   pl.BlockSpec((None, t_valid, IDX_DIM), lambda b: (b, 0, 0)),
                   pl.BlockSpec((None, N_KV_HEADS, tp, HEAD_DIM), lambda b: (b, 0, 0, 0)),
                   pl.BlockSpec((None, KV_W, tp), lambda b: (b, 0, 0)),
                   pl.BlockSpec((None, tp, 4 * IDX_DIM), lambda b: (b, 0, 0))],
        out_shape=[jax.ShapeDtypeStruct((n_batch, t_valid, KV_W), F32),
                   jax.ShapeDtypeStruct((n_batch, t_valid, KV_W), F32),
                   jax.ShapeDtypeStruct((n_batch, t_valid, IDX_DIM), F32),
                   jax.ShapeDtypeStruct((n_batch, N_KV_HEADS, tp, HEAD_DIM), BF16),
                   jax.ShapeDtypeStruct((n_batch, KV_W, tp), BF16),
                   jax.ShapeDtypeStruct((n_batch, tp, 4 * IDX_DIM), BF16)],
        compiler_params=_params("parallel"),
        name="kvprep",
    )(z, z, z, k_norm, idx_k_norm)


_INT_MIN = np.int32(-2 ** 31)


def _sort_key_of(x):
    b = np.array(x, np.float32).view(np.int32)
    return np.int32(b ^ 0x7FFFFFFF) if b < 0 else np.int32(b)


_OK_KEY = _sort_key_of(0.5 * NEG)


def _dsa_kernel(aq_ref, iq_ref, sm_ref, k_ref, v_ref, ki_ref, qn_ref, o_ref, key_ref, msk_ref,
                *, tq, lp, l_valid, top_k, prompt):
    nblk = lp // LANES
    kf = np.float32(top_k)

    iq = iq_ref[...]
    sm = sm_ref[...]
    ki = ki_ref[...]
    sc = None
    for h in range(N_IDX_HEADS):
        s = _dot3_nt(iq[:, h * IDX_DIM:(h + 1) * IDX_DIM], ki)
        s = jnp.maximum(s * np.float32(IDX_DIM ** -0.5), 0.0)
        w = sm[:, SM_IW + h:SM_IW + h + 1] * np.float32(N_IDX_HEADS ** -0.5)
        sc = s * w if sc is None else sc + s * w
    sc = jnp.where(sc == 0.0, 0.0, sc)

    kpos = lax.broadcasted_iota(I32, (1, lp), 1)
    if prompt:
        qpos = pl.program_id(1) * tq + lax.broadcasted_iota(I32, (tq, 1), 0)
        allowed = ((kpos + (CHUNK - N_META)) // CHUNK <= (qpos + (CHUNK - N_META)) // CHUNK) & (kpos < l_valid)
    else:
        allowed = jnp.broadcast_to(kpos < l_valid, (tq, lp))
    scm = jnp.where(allowed, sc, np.float32(NEG))
    bits = pltpu.bitcast(scm, I32)
    key_ref[...] = jnp.where(bits < 0, bits ^ np.int32(0x7FFFFFFF), bits)

    def count_ge(cand):
        acc = jnp.zeros((tq, LANES), F32)
        for j in range(nblk):
            acc = acc + jnp.where(key_ref[:, j * LANES:(j + 1) * LANES] >= cand, 1.0, 0.0)
        return jnp.sum(acc, axis=-1, keepdims=True)

    ans = jnp.where(count_ge(jnp.zeros((tq, 1), I32)) >= kf, np.int32(0), _INT_MIN)

    def bit_step(t, ans):
        cand = ans | lax.shift_left(np.int32(1), np.int32(30) - t)
        return jnp.where(count_ge(cand) >= kf, cand, ans)

    thr = lax.fori_loop(0, 31, bit_step, ans)

    acc = jnp.zeros((tq, LANES), F32)
    for j in range(nblk):
        acc = acc + jnp.where(key_ref[:, j * LANES:(j + 1) * LANES] > thr, 1.0, 0.0)
    need = kf - jnp.sum(acc, axis=-1, keepdims=True)
    tri = (lax.broadcasted_iota(I32, (LANES, LANES), 0) < lax.broadcasted_iota(I32, (LANES, LANES), 1))
    tri = jnp.where(tri, 1.0, 0.0).astype(BF16)
    seen = jnp.zeros((tq, 1), F32)
    for j in range(nblk):
        kj = key_ref[:, j * LANES:(j + 1) * LANES]
        eq = jnp.where(kj == thr, 1.0, 0.0)
        before = seen + jnp.dot(eq.astype(BF16), tri, preferred_element_type=F32)
        sel = (kj > thr) | ((kj == thr) & (before < need))
        sel = sel & (kj > _OK_KEY)
        msk_ref[:, j * LANES:(j + 1) * LANES] = jnp.where(sel, 1.0, 0.0)
        seen = seen + jnp.sum(eq, axis=-1, keepdims=True)

    q = aq_ref[...]
    qn = qn_ref[...]
    kk = k_ref[...]
    vv = v_ref[...]
    sel = msk_ref[...] > 0.5
    outs = []
    for g in range(N_KV_HEADS):
        kg = kk[:, g * HEAD_DIM:(g + 1) * HEAD_DIM].astype(BF16)
        vg = vv[:, g * HEAD_DIM:(g + 1) * HEAD_DIM].astype(BF16)
        for hh in range(GROUP):
            h = g * GROUP + hh
            qh = _rms_rows(q[:, h * HEAD_DIM:(h + 1) * HEAD_DIM], qn)
            s = _bdot_nt(qh, kg) * np.float32(HEAD_DIM ** -0.5)
            s = jnp.where(sel, s, np.float32(NEG))
            p = jnp.exp(s - jnp.max(s, axis=-1, keepdims=True))
            den = jnp.sum(p, axis=-1, keepdims=True)
            outs.append(jnp.dot(p.astype(BF16), vg, preferred_element_type=F32) / den)
    o_ref[...] = jnp.concatenate(outs, axis=-1)


def _dsa(z, off, kmat, vmat, kimat, q_norm, layer, *, n_batch, rows_per_batch, tq, n_q, l_valid, top_k, prompt):
    lp = kmat.shape[1]
    assert lp % LANES == 0 and rows_per_batch % tq == 0
    rb = rows_per_batch // tq
    cb = lambda name: off[name][0] // off[name][1]
    kern = functools.partial(_dsa_kernel, tq=tq, lp=lp, l_valid=l_valid, top_k=top_k, prompt=prompt)
    return pl.pallas_call(
        kern,
        grid=(n_batch, n_q),
        in_specs=[pl.BlockSpec((tq, ATT_W), lambda b, i: (b * rb + i, cb("aq"))),
                  pl.BlockSpec((tq, N_IDX_HEADS * IDX_DIM), lambda b, i: (b * rb + i, cb("iq"))),
                  pl.BlockSpec((tq, SM_W), lambda b, i: (b * rb + i, cb("sm"))),
                  pl.BlockSpec((None, lp, KV_W), lambda b, i: (b, 0, 0)),
                  pl.BlockSpec((None, lp, KV_W), lambda b, i: (b, 0, 0)),
                  pl.BlockSpec((None, lp, IDX_DIM), lambda b, i: (b, 0, 0)),
                  pl.BlockSpec((None, 1, HEAD_DIM), lambda b, i: (layer, 0, 0))],
        out_specs=pl.BlockSpec((tq, ATT_W), lambda b, i: (b * n_q + i, 0)),
        out_shape=jax.ShapeDtypeStruct((n_batch * n_q * tq, ATT_W), F32),
        scratch_shapes=[pltpu.VMEM((tq, lp), I32), pltpu.VMEM((tq, lp), F32)],
        compiler_params=_params("parallel", "parallel"),
        name="dsa",
    )(z, z, z, kmat, vmat, kimat, q_norm)


def _row_blocks(x):
    return [x[j * LANES:(j + 1) * LANES, :] for j in range(x.shape[0] // LANES)]


def _col_reduce(blocks, op, red):
    a = blocks[0]
    for blk in blocks[1:]:
        a = op(a, blk)
    r = a.shape[0]
    while r > 8:
        r //= 2
        a = op(a[:r], a[r:])
    return red(a, axis=0, keepdims=True)


_WORD = 32
_SUB = 8


def _bit_planes(slabs):
    x = list(slabs)
    for d, m in ((16, 0x0000FFFF), (8, 0x00FF00FF), (4, 0x0F0F0F0F), (2, 0x33333333), (1, 0x55555555)):
        for i in range(_WORD):
            if not i & d:
                a, b = x[i], x[i + d]
                t = (lax.shift_right_logical(a, np.int32(d)) ^ b) & np.int32(m)
                x[i + d] = b ^ t
                x[i] = a ^ lax.shift_left(t, np.int32(d))
    return x


def _kth_largest(key_ref, lp, tq, top_k):
    n_pieces = lp // _SUB
    n_grp = -(-n_pieces // _WORD)
    slabs = []
    for i in range(_WORD):
        parts = []
        for g in range(n_grp):
            r0 = (g * _WORD + i) * _SUB
            parts.append(key_ref[r0:r0 + _SUB, :] ^ _INT_MIN if r0 < lp else jnp.zeros((_SUB, tq), I32))
        slabs.append(jnp.concatenate(parts, axis=0) if n_grp > 1 else parts[0])
    planes = _bit_planes(slabs)
    grp = lax.broadcasted_iota(I32, (n_grp * _SUB, tq), 0) // _SUB
    last = n_pieces - (n_grp - 1) * _WORD
    last_mask = np.int32(-1) if last == _WORD else np.int32((1 << last) - 1)
    act = jnp.where(grp < n_grp - 1, np.int32(-1), last_mask)
    above = jnp.zeros((1, tq), I32)
    thr_u = jnp.zeros((1, tq), I32)
    for b in range(_WORD - 1, -1, -1):
        ones = act & planes[b]
        n1 = jnp.sum(lax.population_count(ones), axis=0, keepdims=True)
        take = above + n1 >= top_k
        act = jnp.where(take, ones, act ^ ones)
        above = jnp.where(take, above, above + n1)
        thr_u = jnp.where(take, thr_u | (_INT_MIN if b == _WORD - 1 else np.int32(1 << b)), thr_u)
    n_eq = jnp.sum(lax.population_count(act), axis=0, keepdims=True)
    return thr_u ^ _INT_MIN, above + n_eq


def _order_key(bits):
    return jnp.where(bits < 0, _INT_MIN - bits, bits)


_OK_KEY_T = np.int32(-int(np.array(-0.5 * NEG, np.float32).view(np.int32)))


def _dsa_t_kernel(aq_ref, iq_ref, sm_ref, kb_ref, vt_ref, kic_ref, qn_ref, buf_ref, o_ref, key_ref, bias_ref,
                  *, tq, lp, l_valid, top_k, q0):
    del buf_ref
    kf = np.float32(top_k)
    nt = (((1,), (1,)), ((), ()))

    smt = sm_ref[...].T
    iq = iq_ref[...]
    pieces = []
    for h in range(N_IDX_HEADS):
        qh = iq[:, h * IDX_DIM:(h + 1) * IDX_DIM]
        hi = qh.astype(BF16).astype(F32)
        pieces.append(jnp.concatenate([hi, hi, qh - hi, jnp.zeros_like(hi)], axis=-1))
    qcat = jnp.concatenate(pieces, axis=0).astype(BF16)
    st = lax.dot_general(kic_ref[...], qcat, nt, preferred_element_type=F32)
    sc = None
    for h in range(N_IDX_HEADS):
        w = smt[SM_IW + h:SM_IW + h + 1, :] * np.float32(N_IDX_HEADS ** -0.5 * IDX_DIM ** -0.5)
        t = jnp.maximum(st[:, h * tq:(h + 1) * tq], 0.0) * w
        sc = t if sc is None else sc + t

    kpos = lax.broadcasted_iota(I32, (lp, 1), 0)
    qpos = (q0 + pl.program_id(1)) * tq + lax.broadcasted_iota(I32, (1, tq), 1)
    allowed = ((kpos + (CHUNK - N_META)) // CHUNK <= (qpos + (CHUNK - N_META)) // CHUNK) & (kpos < l_valid)
    scm = jnp.where(allowed, sc, np.float32(NEG))
    key_ref[...] = _order_key(pltpu.bitcast(scm, I32))

    thr, n_ge = _kth_largest(key_ref, lp, tq, top_k)
    exact = (jnp.min(jnp.where(n_ge == top_k, 1.0, 0.0)) > 0.5).astype(I32)

    @pl.when(exact == 1)
    def _():
        key = key_ref[...]
        bias_ref[...] = jnp.where((key >= thr) & (key > _OK_KEY_T), 0.0, np.float32(NEG))

    @pl.when(exact == 0)
    def _():
        need = kf - _col_reduce([jnp.where(key_ref[j * LANES:(j + 1) * LANES, :] > thr, 1.0, 0.0)
                                 for j in range(lp // LANES)], jnp.add, jnp.sum)
        tri = (lax.broadcasted_iota(I32, (LANES, LANES), 1) < lax.broadcasted_iota(I32, (LANES, LANES), 0))
        tri = jnp.where(tri, 1.0, 0.0).astype(BF16)
        seen = jnp.zeros((1, tq), F32)
        for j in range(lp // LANES):
            kj = key_ref[j * LANES:(j + 1) * LANES, :]
            eq = jnp.where(kj == thr, 1.0, 0.0)
            before = seen + jnp.dot(tri, eq.astype(BF16), preferred_element_type=F32)
            sel = ((kj > thr) | ((kj == thr) & (before < need))) & (kj > _OK_KEY_T)
            bias_ref[j * LANES:(j + 1) * LANES, :] = jnp.where(sel, 0.0, np.float32(NEG))
            seen = seen + jnp.sum(eq, axis=0, keepdims=True)

    q = aq_ref[...]
    qn = qn_ref[...] * np.float32(HEAD_DIM ** -0.5 * np.log2(np.e))
    bias = bias_ref[...]
    outs = []
    for g in range(N_KV_HEADS):
        qs = [_rms_rows(q[:, h * HEAD_DIM:(h + 1) * HEAD_DIM], qn) for h in range(g * GROUP, (g + 1) * GROUP)]
        qs = jnp.concatenate(qs, axis=0).astype(BF16)
        s_all = lax.dot_general(kb_ref[g], qs, nt, preferred_element_type=F32)
        vtg = vt_ref[g * HEAD_DIM:(g + 1) * HEAD_DIM, :]
        for hh in range(GROUP):
            s = s_all[:, hh * tq:(hh + 1) * tq] + bias
            p = jnp.exp2(s - _col_reduce(_row_blocks(s), jnp.maximum, jnp.max))
            den = _col_reduce(_row_blocks(p), jnp.add, jnp.sum)
            outs.append(jnp.dot(vtg, p.astype(BF16), preferred_element_type=F32) / den)
    o_ref[...] = jnp.concatenate(outs, axis=0).T


def _dsa_t(z, off, kb, vt, kic, q_norm, layer, out_buf, *, n_batch, tp, tq, q0, n_q, lp, l_valid, top_k):
    rb = tp // tq
    cb = lambda name: off[name][0] // off[name][1]
    kern = functools.partial(_dsa_t_kernel, tq=tq, lp=lp, l_valid=l_valid, top_k=top_k, q0=q0)
    qrow = lambda b, i: b * rb + q0 + i
    return pl.pallas_call(
        kern,
        grid=(n_batch, n_q),
        in_specs=[pl.BlockSpec((tq, ATT_W), lambda b, i: (qrow(b, i), cb("aq"))),
                  pl.BlockSpec((tq, N_IDX_HEADS * IDX_DIM), lambda b, i: (qrow(b, i), cb("iq"))),
                  pl.BlockSpec((tq, SM_W), lambda b, i: (qrow(b, i), cb("sm"))),
                  pl.BlockSpec((None, N_KV_HEADS, lp, HEAD_DIM), lambda b, i: (b, 0, 0, 0)),
                  pl.BlockSpec((None, KV_W, lp), lambda b, i: (b, 0, 0)),
                  pl.BlockSpec((None, lp, 4 * IDX_DIM), lambda b, i: (b, 0, 0)),
                  pl.BlockSpec((None, 1, HEAD_DIM), lambda b, i: (layer, 0, 0)),
                  pl.BlockSpec(memory_space=pl.ANY)],
        out_specs=pl.BlockSpec((None, tq, ATT_W), lambda b, i: (b, q0 + i, 0)),
        out_shape=jax.ShapeDtypeStruct((n_batch, tp, ATT_W), F32),
        input_output_aliases={7: 0},
        scratch_shapes=[pltpu.VMEM((lp, tq), I32), pltpu.VMEM((lp, tq), F32)],
        compiler_params=_params("parallel", "parallel"),
        name="dsa_t",
    )(z, z, z, kb, vt, kic, q_norm, out_buf)


def _mlstm_kernel(bi_ref, bf_ref, mq_ref, mk_ref, mv_ref, mo_ref, sm_ref, gn_ref, c0_ref, n0_ref, m0_ref,
                  yb_ref, c1_ref, n1_ref, m1_ref, c_s, n_s, m_s, *, lc, t_valid, layer, nbb):
    j = pl.program_id(1)

    @pl.when(j == 0)
    def _():
        c_s[...] = c0_ref[...]
        n_s[...] = n0_ref[...]
        m_s[...] = m0_ref[...]

    s_io = lax.broadcasted_iota(I32, (lc, lc), 0)
    t_io = lax.broadcasted_iota(I32, (lc, lc), 1)
    causal = s_io <= t_io
    tri = jnp.where(t_io <= s_io, 1.0, 0.0).astype(BF16)
    tri3 = jnp.concatenate([tri, tri, tri], axis=1)
    pos_col = j * lc + lax.broadcasted_iota(I32, (lc, 1), 0)
    pos_row = j * lc + lax.broadcasted_iota(I32, (1, lc), 1)
    lane = lax.broadcasted_iota(I32, (1, LANES), 1)
    nt = (((1,), (1,)), ((), ()))
    c_in = [[c_s[bb, h] for h in range(M_HEADS)] for bb in range(nbb)]
    n_in = [[n_s[bb, h] for h in range(M_HEADS)] for bb in range(nbb)]
    m_in = [m_s[bb] for bb in range(nbb)]
    c_out, n_out, m_outs, y_out = [], [], [], []
    heads = range(M_HEADS)
    lcat = lambda xs: jnp.concatenate(xs, axis=1)
    causal_h = lcat([causal] * M_HEADS)
    for bb in range(nbb):
        sm = sm_ref[bb]
        smt = sm.T
        mq = mq_ref[bb]
        mk = mk_ref[bb]
        mv = mv_ref[bb]
        mo = mo_ref[bb]
        m_all = m_in[bb]
        i_col = [jnp.where(pos_col < t_valid, sm[:, SM_MI + h:SM_MI + h + 1] + bi_ref[layer, h], np.float32(NEG))
                 for h in heads]
        f_col = [jnp.where(pos_col < t_valid, _log_sigmoid(sm[:, SM_MF + h:SM_MF + h + 1] + bf_ref[layer, h]), 0.0)
                 for h in heads]
        i_row = lcat([jnp.where(pos_row < t_valid, smt[SM_MI + h:SM_MI + h + 1, :] + bi_ref[layer, h],
                                np.float32(NEG)) for h in heads])
        f_b = lcat([jnp.broadcast_to(f_col[h], (lc, lc)) for h in heads])
        i_b = lcat([jnp.broadcast_to(i_col[h], (lc, lc)) for h in heads])
        b_row = jnp.sum(jnp.where(causal_h, f_b, 0.0), axis=0, keepdims=True)
        p1, p2, p3 = _split3(f_b)
        b_src = jnp.dot(tri3, jnp.concatenate([p1, p2, p3], axis=0), preferred_element_type=F32)
        m_prev = lcat([jnp.broadcast_to(m_all[:, h:h + 1], (1, lc)) for h in heads])
        log_d = jnp.where(causal_h, b_row - b_src + i_b, np.float32(NEG))
        m_t = jnp.maximum(b_row + m_prev, jnp.max(log_d, axis=0, keepdims=True))
        d = jnp.exp(log_d - m_t)
        inter = jnp.exp(b_row + m_prev - m_t)
        qh = [mq[:, h * M_QK:(h + 1) * M_QK].astype(BF16) for h in heads]
        kh = [(mk[:, h * M_QK:(h + 1) * M_QK] * np.float32(M_QK ** -0.5)).astype(BF16) for h in heads]
        vt = [mv[:, h * M_V:(h + 1) * M_V].T for h in heads]
        w = lcat([lax.dot_general(kh[h], qh[h], nt, preferred_element_type=F32) for h in heads]) * d
        cq = lcat([lax.dot_general(
            jnp.concatenate([c_in[bb][h], jnp.broadcast_to(n_in[bb][h], (_SUB, M_QK))], axis=0).astype(BF16),
            qh[h], nt, preferred_element_type=F32) for h in heads])
        wb = w.astype(BF16)
        num = lcat([jnp.dot(vt[h].astype(BF16), wb[:, h * lc:(h + 1) * lc], preferred_element_type=F32)
                    for h in heads]) + inter * cq[:M_V]
        den = jnp.sum(w, axis=0, keepdims=True) + inter * cq[M_V:M_V + 1]
        hm = num / jnp.maximum(jnp.abs(den), jnp.exp(-m_t))
        last = lambda r: lcat([jnp.broadcast_to(r[:, (h + 1) * lc - 1:(h + 1) * lc], (1, lc)) for h in heads])
        m_new = last(m_t)
        b_last = last(b_row)
        decay = jnp.exp(b_last + m_prev - m_new)
        ws = jnp.exp(b_last - b_row + i_row - m_new)
        ws_hi = ws.astype(BF16).astype(F32)
        ws_lo = ws - ws_hi
        y = hm * lax.rsqrt(jnp.mean(hm * hm, axis=0, keepdims=True) + EPS) * gn_ref[...]
        m_out = m_all
        for h in heads:
            blk = slice(h * lc, (h + 1) * lc)
            upd = jnp.concatenate([vt[h] * ws[:, blk], ws_hi[:, blk], ws_lo[:, blk],
                                   jnp.zeros((_SUB - 2, lc), F32)], axis=0)
            upd = jnp.dot(upd.astype(BF16), kh[h], preferred_element_type=F32)
            dec = decay[:, h * lc:h * lc + 1]
            c_out.append(dec * c_in[bb][h] + upd[:M_V])
            n_out.append(dec * n_in[bb][h] + upd[M_V:M_V + 1] + upd[M_V + 1:M_V + 2])
            m_out = jnp.where(lane == h, m_new[:, h * lc:h * lc + 1], m_out)
            y_out.append(y[:, blk].T * _sigmoid(mo[:, h * M_V:(h + 1) * M_V]))
        m_outs.append(m_out)
    for bb in range(nbb):
        for h in range(M_HEADS):
            c_s[bb, h] = c_out[bb * M_HEADS + h]
            n_s[bb, h] = n_out[bb * M_HEADS + h]
            yb_ref[bb, :, h * M_V:(h + 1) * M_V] = y_out[bb * M_HEADS + h]
        m_s[bb] = m_outs[bb]

    @pl.when(j == pl.num_programs(1) - 1)
    def _():
        c1_ref[...] = c_s[...]
        n1_ref[...] = n_s[...]
        m1_ref[...] = m_s[...]


def _mlstm(z, off, bias_i, bias_f, m_norm, c0, n0, m0, layer, *, n_batch, tp, t_valid, lc, nbb):
    assert n_batch % nbb == 0 and lc == LANES
    nc = tp // lc
    z3 = z.reshape(n_batch, tp, z.shape[-1])
    cb = lambda name: off[name][0] // off[name][1]
    row = lambda name: pl.BlockSpec((nbb, lc, off[name][1]), lambda b, j: (b, j, cb(name)))
    smem = pl.BlockSpec(memory_space=pltpu.SMEM)
    st_c = pl.BlockSpec((nbb, M_HEADS, M_V, M_QK), lambda b, j: (b, 0, 0, 0))
    st_n = pl.BlockSpec((nbb, M_HEADS, 1, M_QK), lambda b, j: (b, 0, 0, 0))
    st_m = pl.BlockSpec((nbb, 1, LANES), lambda b, j: (b, 0, 0))
    kern = functools.partial(_mlstm_kernel, lc=lc, t_valid=t_valid, layer=layer, nbb=nbb)
    yb, c1, n1, m1 = pl.pallas_call(
        kern,
        grid=(n_batch // nbb, nc),
        in_specs=[smem, smem, row("mq"), row("mk"), row("mv"), row("mo"), row("sm"),
                  pl.BlockSpec((None, M_V, M_HEADS * LANES), lambda b, j: (layer, 0, 0)),
                  st_c, st_n, st_m],
        out_specs=[pl.BlockSpec((nbb, lc, M_HEADS * M_V), lambda b, j: (b, j, 0)), st_c, st_n, st_m],
        out_shape=[jax.ShapeDtypeStruct((n_batch, tp, M_HEADS * M_V), F32),
                   jax.ShapeDtypeStruct(c0.shape, F32), jax.ShapeDtypeStruct(n0.shape, F32),
                   jax.ShapeDtypeStruct(m0.shape, F32)],
        scratch_shapes=[pltpu.VMEM((nbb, M_HEADS, M_V, M_QK), F32), pltpu.VMEM((nbb, M_HEADS, 1, M_QK), F32),
                        pltpu.VMEM((nbb, 1, LANES), F32)],
        compiler_params=_params("parallel", "arbitrary"),
        name="mlstm",
    )(bias_i, bias_f, z3, z3, z3, z3, z3, m_norm, c0, n0, m0)
    return yb.reshape(n_batch * tp, M_HEADS * M_V), c1, n1, m1


def _shift_rows_in_group(x, s, fill):
    rolled = pltpu.roll(x, s, 0)
    row = lax.broadcasted_iota(I32, x.shape, 0)
    return jnp.where((row & 7) >= s, rolled, fill)


def _lru_kernel(lx_ref, lg_ref, cw_ref, cb_ref, w3_ref, ba_ref, bx_ref, lam_ref, h0_ref, buf0_ref,
                yc_ref, h1_ref, buf1_ref, prev_s, h_s, *, tt, t_valid):
    j = pl.program_id(1)

    @pl.when(j == 0)
    def _():
        prev_s[...] = buf0_ref[...]
        h_s[...] = h0_ref[...]

    lx = lx_ref[...]
    cw = cw_ref[...]
    prev = prev_s[...]
    row8 = lax.broadcasted_iota(I32, prev.shape, 0)
    xc = cb_ref[...] + lx * cw[CONV_W - 1:CONV_W, :]
    for s in range(1, CONV_W):
        rolled = pltpu.roll(lx, s, 0)
        head = jnp.where(row8 < s, pltpu.roll(prev, s, 0), rolled[:8])
        shifted = jnp.concatenate([head, rolled[8:]], axis=0) if tt > 8 else head
        xc = xc + shifted * cw[CONV_W - 1 - s:CONV_W - s, :]
    prev_s[...] = lx[tt - 8:, :]

    x_hi = xc.astype(BF16)
    x_lo = (xc - x_hi.astype(F32)).astype(BF16)
    gates = jnp.dot(jnp.concatenate([x_hi, x_hi, x_lo], axis=1), w3_ref[...], preferred_element_type=F32)
    r = _sigmoid(gates[:, :LRU_W] + ba_ref[...])
    gi = _sigmoid(gates[:, LRU_W:] + bx_ref[...])
    log_a = np.float32(-LRU_C) * r * _softplus(-lam_ref[...])
    a = jnp.exp(log_a)
    u = jnp.sqrt(jnp.tanh(-log_a) * (a * a + 1.0)) * gi * xc
    s = 1
    while s < 8:
        u = a * _shift_rows_in_group(u, s, 0.0) + u
        a = a * _shift_rows_in_group(a, s, 1.0)
        s *= 2
    carry = h_s[...]
    hs = []
    for g in range(tt // 8):
        hg = a[g * 8:(g + 1) * 8, :] * carry + u[g * 8:(g + 1) * 8, :]
        carry = hg[7:8, :]
        hs.append(hg)
    h = jnp.concatenate(hs, axis=0)
    h_s[...] = carry
    yc_ref[...] = h * _gelu_tanh(lg_ref[...])

    last = t_valid - 1

    @pl.when(j == last // tt)
    def _():
        r0 = last % tt
        h1_ref[...] = h[r0:r0 + 1, :]
        buf1_ref[...] = lx[r0 - 7:r0 + 1, :]


def _lru(z, off, conv_w, conv_b, w3, ba, bx, lam, h0, buf0, layer, *, n_batch, tp, t_valid, tt):
    assert t_valid % 8 == 0 and tt % 8 == 0
    nt = tp // tt
    cb = lambda name: off[name][0] // off[name][1]
    row = lambda name: pl.BlockSpec((tt, LRU_W), lambda b, j: (b * nt + j, cb(name)))
    vec = lambda n: pl.BlockSpec((None, n, LRU_W), lambda b, j: (layer, 0, 0))
    st_h = pl.BlockSpec((None, 1, LRU_W), lambda b, j: (b, 0, 0))
    st_b = pl.BlockSpec((None, 8, LRU_W), lambda b, j: (b, 0, 0))
    kern = functools.partial(_lru_kernel, tt=tt, t_valid=t_valid)
    return pl.pallas_call(
        kern,
        grid=(n_batch, nt),
        in_specs=[row("lx"), row("lg"), vec(CONV_W), vec(1),
                  pl.BlockSpec((None, 3 * LRU_W, 2 * LRU_W), lambda b, j: (layer, 0, 0)), vec(1), vec(1), vec(1),
                  st_h, st_b],
        out_specs=[pl.BlockSpec((tt, LRU_W), lambda b, j: (b * nt + j, 0)), st_h, st_b],
        out_shape=[jax.ShapeDtypeStruct((n_batch * tp, LRU_W), F32),
                   jax.ShapeDtypeStruct(h0.shape, F32), jax.ShapeDtypeStruct(buf0.shape, F32)],
        scratch_shapes=[pltpu.VMEM((8, LRU_W), F32), pltpu.VMEM((1, LRU_W), F32)],
        compiler_params=_params("parallel", "arbitrary"),
        name="lru",
    )(z, z, conv_w, conv_b, w3, ba, bx, lam, h0, buf0)


def _merge_kernel(x_ref, ya_ref, yb_ref, yc_ref, gt_ref, wb_ref, wo_ref, o_ref, *, d):
    acc = None
    for b, y_ref in enumerate((ya_ref, yb_ref, yc_ref)):
        proj = jnp.dot(y_ref[...].astype(BF16), wb_ref[b], preferred_element_type=F32)
        t = _sigmoid(gt_ref[:, b * d:(b + 1) * d]) * proj
        acc = t if acc is None else acc + t
    o_ref[...] = x_ref[...] + jnp.dot(acc.astype(BF16), wo_ref[...], preferred_element_type=F32)


def _merge(x, ya, yb, yc, z, off, w_branch, w_out, layer, *, tm):
    rows, d = x.shape
    assert off["gt"][0] == 0
    rowspec = lambda w: pl.BlockSpec((tm, w), lambda i: (i, 0))
    return pl.pallas_call(
        functools.partial(_merge_kernel, d=d),
        grid=(rows // tm,),
        in_specs=[rowspec(d), rowspec(BRANCH_W), rowspec(BRANCH_W), rowspec(BRANCH_W), rowspec(N_BRANCH * d),
                  pl.BlockSpec((None, N_BRANCH, BRANCH_W, d), lambda i: (layer, 0, 0, 0)),
                  pl.BlockSpec((None, d, d), lambda i: (layer, 0, 0))],
        out_specs=rowspec(d),
        out_shape=jax.ShapeDtypeStruct((rows, d), F32),
        compiler_params=_params("parallel"),
        name="merge",
    )(x, ya, yb, yc, z, w_branch, w_out)


def _ffn_kernel(x_ref, g_ref, wu_ref, wd_ref, o_ref, hf_s, acc_s):
    j = pl.program_id(1)

    @pl.when(j == 0)
    def _():
        hf_s[...] = _rms_rows(x_ref[...], g_ref[...]).astype(BF16)
        acc_s[...] = jnp.zeros_like(acc_s)

    up = jnp.dot(hf_s[...], wu_ref[...], preferred_element_type=F32)
    act = jnp.square(jnp.maximum(up, 0.0))
    acc_s[...] += jnp.dot(act.astype(BF16), wd_ref[...], preferred_element_type=F32)

    @pl.when(j == pl.num_programs(1) - 1)
    def _():
        o_ref[...] = x_ref[...] + acc_s[...]


def _ffn(x, g, w_up, w_down, layer, *, tm, tf):
    rows, d = x.shape
    dff = w_up.shape[-1]
    return pl.pallas_call(
        _ffn_kernel,
        grid=(rows // tm, dff // tf),
        in_specs=[pl.BlockSpec((tm, d), lambda i, j: (i, 0)),
                  pl.BlockSpec((None, 1, d), lambda i, j: (layer, 0, 0)),
                  pl.BlockSpec((None, d, tf), lambda i, j: (layer, 0, j)),
                  pl.BlockSpec((None, tf, d), lambda i, j: (layer, j, 0))],
        out_specs=pl.BlockSpec((tm, d), lambda i, j: (i, 0)),
        out_shape=jax.ShapeDtypeStruct((rows, d), F32),
        scratch_shapes=[pltpu.VMEM((tm, d), BF16), pltpu.VMEM((tm, d), F32)],
        compiler_params=_params("parallel", "arbitrary"),
        name="ffn",
    )(x, g, w_up, w_down)


def _final_kernel(x_ref, g_ref, o_ref, *, r0, n_out):
    o_ref[...] = _rms_rows(x_ref[r0:r0 + n_out, :], g_ref[...])


def _final_norm(x3, g, *, r0, n_out):
    nb, tp, d = x3.shape
    return pl.pallas_call(
        functools.partial(_final_kernel, r0=r0, n_out=n_out),
        grid=(nb,),
        in_specs=[pl.BlockSpec((None, tp, d), lambda b: (b, 0, 0)), pl.BlockSpec((1, d), lambda b: (0, 0))],
        out_specs=pl.BlockSpec((None, n_out, d), lambda b: (b, 0, 0)),
        out_shape=jax.ShapeDtypeStruct((nb, n_out, d), F32),
        compiler_params=_params("parallel"),
        name="final_norm",
    )(x3, g)


def _pick_tile(n, pref):
    t = pref
    while n % t:
        t //= 2
    return t


def _layer(x, p, layer, st, *, n_batch, tp, t_valid, prompt, ya_buf=None):
    d = x.shape[1]
    rows = n_batch * tp
    off = p["off"]
    tm = _pick_tile(rows, 512)
    z = _inproj(x, p["norm_mix"], p["w_in"], layer, tm=tm, n_col_tiles=p["n_col_tiles"])
    k_new, v_new, ki_new, kb, vt, kic = _kvprep(z, off, p["k_norm"], p["idx_k_norm"], layer, n_batch=n_batch,
                                                tp=tp, t_valid=t_valid)

    if prompt:
        top_k = min(TOP_K_MAX, (t_valid - N_META) // 4)
        tq, n_q = ROW_PAD, tp // ROW_PAD
        ya3, i0 = ya_buf, 0
        while i0 < n_q:
            i1 = n_q if n_q - i0 <= DSA_TILES_PER_GROUP + 1 else i0 + DSA_TILES_PER_GROUP
            lp = min(tp, -(-(i1 * tq + CHUNK) // LANES) * LANES)
            assert lp >= top_k
            ya3 = _dsa_t(z, off, kb, vt, kic, p["q_norm"], layer, ya3, n_batch=n_batch, tp=tp, tq=tq, q0=i0,
                         n_q=i1 - i0, lp=lp, l_valid=t_valid, top_k=top_k)
            i0 = i1
        ya = ya3.reshape(rows, ATT_W)
        ya_buf = ya3
    else:
        past = st[0].shape[1]
        l_valid = past + t_valid
        lp = -(-l_valid // LANES) * LANES
        cat = lambda old, new: jnp.concatenate(
            [old.reshape(n_batch, past, -1).astype(F32), new,
             jnp.zeros((n_batch, lp - l_valid, new.shape[-1]), F32)], axis=1)
        kmat, vmat, kimat = cat(st[0], k_new), cat(st[1], v_new), cat(st[2], ki_new)
        top_k = min(TOP_K_MAX, l_valid // 4)
        tq = t_valid
        ya = _dsa(z, off, kmat, vmat, kimat, p["q_norm"], layer, n_batch=n_batch, rows_per_batch=tp, tq=tq,
                  n_q=1, l_valid=l_valid, top_k=top_k, prompt=False)
        ya = jnp.pad(ya.reshape(n_batch, tq, ATT_W), ((0, 0), (0, tp - tq), (0, 0))).reshape(rows, ATT_W)

    if prompt:
        c0 = jnp.zeros((n_batch, M_HEADS, M_V, M_QK), F32)
        n0 = jnp.zeros((n_batch, M_HEADS, 1, M_QK), F32)
        m0 = jnp.zeros((n_batch, 1, LANES), F32)
        h0 = jnp.zeros((n_batch, 1, LRU_W), F32)
        buf0 = jnp.zeros((n_batch, 8, LRU_W), F32)
    else:
        c0 = st[3].astype(F32)
        n0 = st[4].astype(F32).reshape(n_batch, M_HEADS, 1, M_QK)
        m0 = jnp.pad(st[5].astype(F32), ((0, 0), (0, LANES - M_HEADS))).reshape(n_batch, 1, LANES)
        h0 = st[6].astype(F32).reshape(n_batch, 1, LRU_W)
        buf0 = jnp.pad(st[7].astype(F32), ((0, 0), (8 - (CONV_W - 1), 0), (0, 0)))
    yb, c1, n1, m1 = _mlstm(z, off, p["m_bias_i"], p["m_bias_f"], p["m_norm"], c0, n0, m0, layer,
                            n_batch=n_batch, tp=tp, t_valid=t_valid, lc=ROW_PAD, nbb=_pick_tile(n_batch, MLSTM_SEQS))
    yc, h1, buf1 = _lru(z, off, p["conv_w"], p["conv_b"], p["lru_w3"], p["lru_ba"], p["lru_bx"],
                        p["lru_lambda"], h0, buf0, layer, n_batch=n_batch, tp=tp, t_valid=t_valid, tt=ROW_PAD)

    x = _merge(x, ya, yb, yc, z, off, p["w_branch"], p["w_out"], layer, tm=tm)
    x = _ffn(x, p["norm_ffn"], p["w_up"], p["w_down"], layer, tm=tm, tf=_pick_tile(p["w_up"].shape[-1], 2048))
    state = (k_new.reshape(n_batch, t_valid, N_KV_HEADS, HEAD_DIM),
             v_new.reshape(n_batch, t_valid, N_KV_HEADS, HEAD_DIM), ki_new,
             c1, n1.reshape(n_batch, M_HEADS, M_QK), m1[:, 0, :M_HEADS],
             h1.reshape(n_batch, LRU_W), buf1[:, 8 - (CONV_W - 1):])
    return x, state, ya_buf


def _block_diag(w):
    depth, nb, c, _ = w.shape
    eye = jnp.eye(nb, dtype=w.dtype)
    return jnp.einsum("lncd,nm->lncmd", w, eye).reshape(depth, nb * c, nb * c)


def _gate_weights(wa, wx):
    w = jnp.concatenate([_block_diag(wa.astype(F32)), _block_diag(wx.astype(F32))], axis=-1)
    hi = w.astype(BF16)
    lo = (w - hi.astype(F32)).astype(BF16)
    return jnp.concatenate([hi, lo, hi], axis=1)


def kernel(x_prompt, x_sample, cache_k, cache_v, cache_idx_k, state_mlstm_c, state_mlstm_n, state_mlstm_m, state_lru_h, state_conv, meta_tokens, norm_mix, w_in, q_norm, k_norm, idx_k_norm, m_bias_i, m_bias_f, m_norm, conv_w, conv_b, lru_wa, lru_ba, lru_wx, lru_bx, lru_lambda, w_branch, w_out, norm_ffn, w_up, w_down, final_norm):
    depth = w_in.shape[0]
    nb, seq, d = x_prompt.shape
    nbs, dseq, _ = x_sample.shape
    n_col_tiles = 3
    w_packed, off, _ = _pack_w_in(w_in, d, n_col_tiles)
    vec3 = lambda a: a.astype(F32).reshape(depth, 1, -1)
    p = dict(off=off, n_col_tiles=n_col_tiles, w_in=w_packed,
             norm_mix=vec3(norm_mix), q_norm=vec3(q_norm), k_norm=vec3(k_norm), idx_k_norm=vec3(idx_k_norm),
             m_bias_i=m_bias_i.astype(F32), m_bias_f=m_bias_f.astype(F32), m_norm=jnp.repeat(jnp.swapaxes(m_norm.astype(F32), 1, 2), LANES, axis=2),
             conv_w=conv_w.astype(F32), conv_b=vec3(conv_b),
             lru_w3=_gate_weights(lru_wa, lru_wx), lru_ba=vec3(lru_ba), lru_bx=vec3(lru_bx),
             lru_lambda=vec3(lru_lambda),
             w_branch=w_branch.astype(BF16), w_out=w_out.astype(BF16), norm_ffn=vec3(norm_ffn),
             w_up=w_up.astype(BF16), w_down=w_down.astype(BF16))

    t_p = N_META + seq
    tp_p = -(-t_p // ROW_PAD) * ROW_PAD
    meta = jnp.broadcast_to(meta_tokens[None].astype(x_prompt.dtype), (nb, N_META, d))
    hp = jnp.concatenate([meta, x_prompt, jnp.zeros((nb, tp_p - t_p, d), x_prompt.dtype)], axis=1)
    hp = hp.reshape(nb * tp_p, d)
    tp_s = -(-dseq // ROW_PAD) * ROW_PAD
    hs = jnp.pad(x_sample, ((0, 0), (0, tp_s - dseq), (0, 0))).reshape(nbs * tp_s, d)

    p_states, s_states = [], []
    ya_buf = jnp.zeros((nb, tp_p, ATT_W), F32)
    for l in range(depth):
        hp, stp, ya_buf = _layer(hp, p, l, None, n_batch=nb, tp=tp_p, t_valid=t_p, prompt=True, ya_buf=ya_buf)
        hs, sts, _ = _layer(hs, p, l, (cache_k[l], cache_v[l], cache_idx_k[l], state_mlstm_c[l], state_mlstm_n[l],
                                       state_mlstm_m[l], state_lru_h[l], state_conv[l]),
                            n_batch=nbs, tp=tp_s, t_valid=dseq, prompt=False)
        p_states.append(stp)
        s_states.append(sts)
    fg = final_norm.astype(F32).reshape(1, d)
    y_prompt = _final_norm(hp.reshape(nb, tp_p, d), fg, r0=N_META, n_out=seq)
    y_sample = _final_norm(hs.reshape(nbs, tp_s, d), fg, r0=0, n_out=dseq)
    pouts = [jnp.stack(t) for t in zip(*p_states)]
    souts = [jnp.stack(t) for t in zip(*s_states)]
    return (y_prompt, y_sample, *pouts, *souts)
```
